```python
import math
import jax
import jax.numpy as jnp
from jax import lax
import numpy as np

D_MODEL = 1024
BATCH = 16
SEQ = 4096
DEPTH = 1

GRID_W = 64
CTX_LEN = 256
EPS = 1e-6
HEAD_DIM = 64
N_Q_HEADS = 16
N_KV_HEADS = 4
GQA_GROUP = N_Q_HEADS // N_KV_HEADS
ATT_W = N_Q_HEADS * HEAD_DIM
KV_W = N_KV_HEADS * HEAD_DIM
WINDOW = 128
ATT_BLOCK = 128
ROPE_BASE = 10000.0
ROPE_FREQS = HEAD_DIM // 4
SSM_EXPAND = 2
D_INNER = SSM_EXPAND * D_MODEL
SSM_HEAD_DIM = 64
N_SSM_HEADS = D_INNER // SSM_HEAD_DIM
N_SSM_GROUPS = 4
HEADS_PER_GROUP = N_SSM_HEADS // N_SSM_GROUPS
D_STATE = 128
BC_W = N_SSM_GROUPS * D_STATE
CONV_K = 5
CONV_CH = D_INNER + 2 * BC_W
CHUNK = 128
D_FF = -(-8 * D_MODEL // (3 * 256)) * 256
N_MOD = 6
SPLIT_SIZES = (ATT_W, KV_W, KV_W, D_INNER, CONV_CH, N_SSM_HEADS, N_SSM_HEADS, D_MODEL, D_MODEL)
N_IN = sum(SPLIT_SIZES)
SPLIT_OFFSETS = tuple(int(o) for o in np.cumsum(SPLIT_SIZES)[:-1])

kernel_name = "hybrid_swa_ssd_adaln_block"


def rmsnorm(x, g):
    xf = x.astype(jnp.float32)
    xf = xf * lax.rsqrt(jnp.mean(jnp.square(xf), axis=-1, keepdims=True) + EPS)
    return xf.astype(x.dtype) * g


def axial_rope(rows, dtype):
    row = jnp.repeat(jnp.arange(rows), GRID_W).astype(jnp.float32)
    col = jnp.tile(jnp.arange(GRID_W), rows).astype(jnp.float32)
    inv = ROPE_BASE ** (-jnp.arange(ROPE_FREQS, dtype=jnp.float32) / ROPE_FREQS)
    ang_r = row[:, None] * inv
    ang_c = col[:, None] * inv
    ang = jnp.concatenate([ang_r, ang_r, ang_c, ang_c], axis=-1)
    return jnp.cos(ang).astype(dtype)[:, None, :], jnp.sin(ang).astype(dtype)[:, None, :]


def apply_rope(x, cos, sin):
    xr = x.reshape(x.shape[:-1] + (2, 2, ROPE_FREQS))
    rot = jnp.concatenate([-xr[..., 1:, :], xr[..., :1, :]], axis=-2).reshape(x.shape)
    return x * cos + rot * sin


def dwconv_centred(u, w, b):
    y = lax.conv_general_dilated(
        u, w[:, None, :].astype(u.dtype), window_strides=(1,),
        padding=[(CONV_K // 2, CONV_K // 2)],
        dimension_numbers=('NWC', 'WIO', 'NWC'),
        feature_group_count=u.shape[-1])
    return y + b


def attend(q, parts, sink):
    scale = HEAD_DIM ** -0.5
    b, nq = q.shape[0], q.shape[1]
    logits = []
    for k, _, mask in parts:
        s = jnp.einsum('bqkgd,bskd->bkgqs', q, k).astype(jnp.float32) * scale
        if mask is not None:
            s = jnp.where(mask, s, -jnp.inf)
        logits.append(s)
    sink_l = sink.astype(jnp.float32).reshape(1, N_KV_HEADS, GQA_GROUP, 1, 1)
    logits.append(jnp.broadcast_to(sink_l, (b, N_KV_HEADS, GQA_GROUP, nq, 1)))
    p = jax.nn.softmax(jnp.concatenate(logits, axis=-1), axis=-1).astype(q.dtype)
    out = None
    off = 0
    for k, v, _ in parts:
        n = k.shape[1]
        o = jnp.einsum('bkgqs,bskd->bqkgd', p[..., off:off + n], v)
        out = o if out is None else out + o
        off += n
    return out


def latent_window_attention(q, k, v, k_ctx, v_ctx, sink):
    b, L = q.shape[0], q.shape[1]
    nb = L // ATT_BLOCK
    span = ATT_BLOCK + 2 * WINDOW
    qb = q.reshape(b, nb, ATT_BLOCK, N_KV_HEADS, GQA_GROUP, HEAD_DIM).swapaxes(0, 1)
    pad = ((0, 0), (WINDOW, WINDOW), (0, 0), (0, 0))
    kp = jnp.pad(k, pad)
    vp = jnp.pad(v, pad)
    r = jnp.arange(ATT_BLOCK)[:, None]
    j = jnp.arange(span)[None, :]
    band = jnp.abs(j - WINDOW - r) <= WINDOW

    def one_block(args):
        q_blk, i = args
        start = i * ATT_BLOCK
        k_win = lax.dynamic_slice_in_dim(kp, start, span, axis=1)
        v_win = lax.dynamic_slice_in_dim(vp, start, span, axis=1)
        kpos = start - WINDOW + jnp.arange(span)
        mask = band & ((kpos >= 0) & (kpos < L))[None, :]
        return attend(q_blk, [(k_win, v_win, mask), (k_ctx, v_ctx, None)], sink)

    o = lax.map(one_block, (qb, jnp.arange(nb)))
    return o.swapaxes(0, 1).reshape(b, L, ATT_W)


def context_attention(q, k, v, sink):
    b, lc = q.shape[0], q.shape[1]
    qg = q.reshape(b, lc, N_KV_HEADS, GQA_GROUP, HEAD_DIM)
    return attend(qg, [(k, v, None)], sink).reshape(b, lc, ATT_W)


def ssd_chunked_scan(x, dt, a, bmat, cmat, state0):
    b, L = x.shape[0], x.shape[1]
    nc = L // CHUNK

    def chunks(t):
        return jnp.moveaxis(t.reshape((b, nc, CHUNK) + t.shape[2:]), 1, 0)

    xs = chunks(x.astype(jnp.float32).reshape(b, L, N_SSM_GROUPS, HEADS_PER_GROUP, SSM_HEAD_DIM))
    dts = chunks(dt.reshape(b, L, N_SSM_GROUPS, HEADS_PER_GROUP))
    bs = chunks(bmat.astype(jnp.float32))
    cs = chunks(cmat.astype(jnp.float32))
    a_g = a.reshape(N_SSM_GROUPS, HEADS_PER_GROUP)
    tril = (jnp.arange(CHUNK)[:, None] >= jnp.arange(CHUNK)[None, :])[None, :, :, None, None]

    def step(state, inp):
        xc, dtc, bc, cc = inp
        acs = jnp.cumsum(dtc * a_g, axis=1)
        seg = acs[:, :, None] - acs[:, None, :]
        decay = jnp.exp(jnp.where(tril, seg, -jnp.inf))
        cb = jnp.einsum('bign,bjgn->bijg', cc, bc)
        xdt = xc * dtc[..., None]
        y_intra = jnp.einsum('bijg,bijgh,bjghp->bighp', cb, decay, xdt)
        y_inter = jnp.einsum('bign,bghpn->bighp', cc, state) * jnp.exp(acs)[..., None]
        decay_end = jnp.exp(acs[:, -1:] - acs)
        new_state = state * jnp.exp(acs[:, -1])[..., None, None] + jnp.einsum(
            'bjgn,bjgh,bjghp->bghpn', bc, decay_end, xdt)
        return new_state, y_intra + y_inter

    final, ys = lax.scan(step, state0, (xs, dts, bs, cs))
    y = jnp.moveaxis(ys, 0, 1).reshape(b, L, N_SSM_HEADS, SSM_HEAD_DIM)
    return y, final


def ssd_bidirectional(xs, dt_f, dt_b, bm, cm, a_f, a_b, state_f, state_b):
    y_f, s_f = ssd_chunked_scan(xs, dt_f, a_f, bm, cm, state_f)
    flip = lambda t: jnp.flip(t, axis=1)
    y_b, s_b = ssd_chunked_scan(flip(xs), flip(dt_b), a_b, flip(bm), flip(cm), state_b)
    return y_f + flip(y_b), s_f, s_b


def project_stream(h, w_in, conv_w, conv_b, dt_bias_f, dt_bias_b):
    b, L = h.shape[0], h.shape[1]
    q, k, v, z, xbc, dtf, dtb, ga, gb = jnp.split(h @ w_in, SPLIT_OFFSETS, axis=-1)
    xbc = jax.nn.silu(dwconv_centred(xbc, conv_w, conv_b))
    xs, bm, cm = jnp.split(xbc, [D_INNER, D_INNER + BC_W], axis=-1)
    q = q.reshape(b, L, N_Q_HEADS, HEAD_DIM)
    k = k.reshape(b, L, N_KV_HEADS, HEAD_DIM)
    v = v.reshape(b, L, N_KV_HEADS, HEAD_DIM)
    xs = xs.reshape(b, L, N_SSM_HEADS, SSM_HEAD_DIM)
    bm = bm.reshape(b, L, N_SSM_GROUPS, D_STATE)
    cm = cm.reshape(b, L, N_SSM_GROUPS, D_STATE)
    dtf = jax.nn.softplus(dtf.astype(jnp.float32) + dt_bias_f.astype(jnp.float32))
    dtb = jax.nn.softplus(dtb.astype(jnp.float32) + dt_bias_b.astype(jnp.float32))
    return q, k, v, z, xs, bm, cm, dtf, dtb, ga, gb


def merge_branches(y_att, y_ssm, xs, z, ga, gb, d_skip, ssm_norm_g, w_oa, w_ob, w_out):
    b, L = xs.shape[0], xs.shape[1]
    y_ssm = y_ssm.astype(xs.dtype) + d_skip[:, None] * xs
    y_ssm = y_ssm.reshape(b, L, D_INNER) * jax.nn.silu(z)
    y_ssm = rmsnorm(y_ssm.reshape(b, L, N_SSM_GROUPS, D_INNER // N_SSM_GROUPS),
                    ssm_norm_g.reshape(N_SSM_GROUPS, D_INNER // N_SSM_GROUPS)).reshape(b, L, D_INNER)
    merged = jax.nn.sigmoid(ga) * (y_att @ w_oa) + jax.nn.sigmoid(gb) * (y_ssm @ w_ob)
    return merged @ w_out


def swiglu(h, w_in, w_out):
    g, u = jnp.split(h @ w_in, 2, axis=-1)
    return (jax.nn.silu(g) * u) @ w_out


def setup_inputs(seed: int = 0) -> dict:
    key = jax.random.key(seed)
    ks = jax.random.split(key, 32)

    def nrm(k, shape, scale):
        return jax.random.normal(k, shape, jnp.float32) * scale

    def dt_bias(k):
        u = jax.random.uniform(k, (DEPTH, N_SSM_HEADS), jnp.float32, math.log(1e-3), math.log(1e-1))
        dt = jnp.exp(u)
        return dt + jnp.log(-jnp.expm1(-dt))

    return {
        'x': nrm(ks[0], (BATCH, SEQ, D_MODEL), 1.0),
        'c': nrm(ks[1], (BATCH, D_MODEL), 1.0),
        'ctx': nrm(ks[2], (BATCH, CTX_LEN, D_MODEL), 1.0),
        'c_ctx': nrm(ks[3], (D_MODEL,), 1.0),
        'w_mod': nrm(ks[4], (DEPTH, D_MODEL, N_MOD * D_MODEL), 0.25 * D_MODEL ** -0.5),
        'b_mod': nrm(ks[5], (DEPTH, N_MOD * D_MODEL), 0.01),
        'norm1_g': 1.0 + nrm(ks[6], (DEPTH, D_MODEL), 0.02),
        'w_in': nrm(ks[7], (DEPTH, D_MODEL, N_IN), D_MODEL ** -0.5),
        'attn_sink': nrm(ks[8], (DEPTH, N_Q_HEADS), 0.5),
        'conv_w': nrm(ks[9], (DEPTH, CONV_K, CONV_CH), CONV_K ** -0.5),
        'conv_b': nrm(ks[10], (DEPTH, CONV_CH), 0.01),
        'a_log_f': jnp.log(jax.random.uniform(ks[11], (DEPTH, N_SSM_HEADS), jnp.float32, 1.0, 16.0)),
        'a_log_b': jnp.log(jax.random.uniform(ks[12], (DEPTH, N_SSM_HEADS), jnp.float32, 1.0, 16.0)),
        'dt_bias_f': dt_bias(ks[13]),
        'dt_bias_b': dt_bias(ks[14]),
        'd_skip': 1.0 + nrm(ks[15], (DEPTH, N_SSM_HEADS), 0.01),
        'ssm_norm_g': 1.0 + nrm(ks[16], (DEPTH, D_INNER), 0.02),
        'w_oa': nrm(ks[17], (DEPTH, ATT_W, D_MODEL), ATT_W ** -0.5),
        'w_ob': nrm(ks[18], (DEPTH, D_INNER, D_MODEL), D_INNER ** -0.5),
        'w_out': nrm(ks[19], (DEPTH, D_MODEL, D_MODEL), D_MODEL ** -0.5),
        'norm2_g': 1.0 + nrm(ks[20], (DEPTH, D_MODEL), 0.02),
        'w_ffn_in': nrm(ks[21], (DEPTH, D_MODEL, 2 * D_FF), D_MODEL ** -0.5),
        'w_ffn_out': nrm(ks[22], (DEPTH, D_FF, D_MODEL), D_FF ** -0.5),
        'final_g': 1.0 + nrm(ks[23], (D_MODEL,), 0.02),
    }


def reference(x, c, ctx, c_ctx, w_mod, b_mod, norm1_g, w_in, attn_sink, conv_w, conv_b,
              a_log_f, a_log_b, dt_bias_f, dt_bias_b, d_skip, ssm_norm_g, w_oa, w_ob, w_out,
              norm2_g, w_ffn_in, w_ffn_out, final_g):
    b, L = x.shape[0], x.shape[1]
    rows = L // GRID_W
    cos, sin = axial_rope(rows, x.dtype)
    c_act = jax.nn.silu(c)
    cc_act = jax.nn.silu(c_ctx)
    xc = ctx
    for li in range(DEPTH):
        mod = (c_act @ w_mod[li] + b_mod[li])[:, None, :]
        mod_c = cc_act @ w_mod[li] + b_mod[li]
        sh1, sc1, g1, sh2, sc2, g2 = jnp.split(mod, N_MOD, axis=-1)
        sh1c, sc1c, g1c, sh2c, sc2c, g2c = jnp.split(mod_c, N_MOD, axis=-1)
        h = rmsnorm(x, norm1_g[li]) * (1.0 + sc1) + sh1
        hc = rmsnorm(xc, norm1_g[li]) * (1.0 + sc1c) + sh1c
        q, k, v, z, xs, bm, cm, dtf, dtb, ga, gb = project_stream(
            h, w_in[li], conv_w[li], conv_b[li], dt_bias_f[li], dt_bias_b[li])
        qc, kc, vc, zc, xsc, bmc, cmc, dtfc, dtbc, gac, gbc = project_stream(
            hc, w_in[li], conv_w[li], conv_b[li], dt_bias_f[li], dt_bias_b[li])
        a_f = -jnp.exp(a_log_f[li].astype(jnp.float32))
        a_b = -jnp.exp(a_log_b[li].astype(jnp.float32))
        zero_state = jnp.zeros((b, N_SSM_GROUPS, HEADS_PER_GROUP, SSM_HEAD_DIM, D_STATE), jnp.float32)
        y_ssm_c, s_f, s_b = ssd_bidirectional(xsc, dtfc, dtbc, bmc, cmc, a_f, a_b, zero_state, zero_state)
        y_ssm, _, _ = ssd_bidirectional(xs, dtf, dtb, bm, cm, a_f, a_b, s_f, s_b)
        q = apply_rope(q, cos, sin)
        k = apply_rope(k, cos, sin)
        y_att = latent_window_attention(q, k, v, kc, vc, attn_sink[li])
        x = x + g1 * merge_branches(y_att, y_ssm, xs, z, ga, gb, d_skip[li], ssm_norm_g[li],
                                    w_oa[li], w_ob[li], w_out[li])
        h2 = rmsnorm(x, norm2_g[li]) * (1.0 + sc2) + sh2
        x = x + g2 * swiglu(h2, w_ffn_in[li], w_ffn_out[li])
        if li < DEPTH - 1:
            y_att_c = context_attention(qc, kc, vc, attn_sink[li])
            xc = xc + g1c * merge_branches(y_att_c, y_ssm_c, xsc, zc, gac, gbc, d_skip[li], ssm_norm_g[li],
                                           w_oa[li], w_ob[li], w_out[li])
            hc2 = rmsnorm(xc, norm2_g[li]) * (1.0 + sc2c) + sh2c
            xc = xc + g2c * swiglu(hc2, w_ffn_in[li], w_ffn_out[li])
    return rmsnorm(x, final_g)
```

```python
import functools

import numpy as np
import jax
import jax.numpy as jnp
from jax import lax
from jax.experimental import pallas as pl
from jax.experimental.pallas import tpu as pltpu

F32 = jnp.float32
BF16 = jnp.bfloat16

D_MODEL = 1024
GRID_W = 64
EPS = 1e-6
HEAD_DIM = 64
N_Q_HEADS = 16
N_KV_HEADS = 4
ATT_W = N_Q_HEADS * HEAD_DIM
KV_W = N_KV_HEADS * HEAD_DIM
WINDOW = 128
ROPE_BASE = 10000.0
ROPE_FREQS = HEAD_DIM // 4
D_INNER = 2 * D_MODEL
SSM_HEAD_DIM = 64
N_SSM_HEADS = D_INNER // SSM_HEAD_DIM
N_SSM_GROUPS = 4
GROUP_W = D_INNER // N_SSM_GROUPS
D_STATE = 128
BC_W = N_SSM_GROUPS * D_STATE
CONV_K = 5
CONV_CH = D_INNER + 2 * BC_W
CHUNK = 128
D_FF = -(-8 * D_MODEL // (3 * 256)) * 256
N_MOD = 6

LANES = 128
SUBLANES = 8
BF16_ROWS = 16
VMEM_LIMIT = 56 * 1024 * 1024

OFF_Q = 0
OFF_K = OFF_Q + ATT_W
OFF_V = OFF_K + KV_W
OFF_Z = OFF_V + KV_W
OFF_XBC = OFF_Z + D_INNER
OFF_GA = OFF_XBC + CONV_CH
OFF_GB = OFF_GA + D_MODEL
OFF_DT = OFF_GB + D_MODEL
DT_W = LANES
N_PROJ = OFF_DT + DT_W
KV_CAT = 4 * N_KV_HEADS * LANES

ROW_TILE = 512
SSD_ROWS = 512
ATT_BLOCK = 128


def _cparams(sem):
    return pltpu.CompilerParams(dimension_semantics=sem, vmem_limit_bytes=VMEM_LIMIT)


def _const_spec(shape):
    n = len(shape)
    return pl.BlockSpec(shape, lambda *_: (0,) * n, pipeline_mode=pl.Buffered(1))


def _dot(a, b):
    return jnp.dot(a, b, preferred_element_type=F32)


def _dot_nt(a, b):
    return lax.dot_general(a, b, (((1,), (1,)), ((), ())), preferred_element_type=F32)


def _dot_tn(a, b):
    return lax.dot_general(a, b, (((0,), (0,)), ((), ())), preferred_element_type=F32)


def _sigmoid(v):
    return 1.0 / (1.0 + jnp.exp(-v))


def _silu(v):
    return v * _sigmoid(v)


def _split2(v):
    hi = v.astype(BF16)
    mid = (v - hi.astype(F32)).astype(BF16)
    return hi, mid


def _split3(v):
    hi = v.astype(BF16)
    r = v - hi.astype(F32)
    mid = r.astype(BF16)
    lo = (r - mid.astype(F32)).astype(BF16)
    return hi, mid, lo


def _mod_kernel(c_ref, w_ref, b_ref, o_ref):
    act = _silu(c_ref[...])
    o_ref[...] = jnp.dot(act, w_ref[...], preferred_element_type=F32,
                         precision=lax.Precision.HIGHEST) + b_ref[...]


def _modulation(c_all, w_mod, b_mod):
    rows, d = c_all.shape
    n = w_mod.shape[1]
    tn = 1024
    return pl.pallas_call(
        _mod_kernel,
        grid=(n // tn,),
        in_specs=[pl.BlockSpec((rows, d), lambda j: (0, 0)),
                  pl.BlockSpec((d, tn), lambda j: (0, j)),
                  pl.BlockSpec((1, tn), lambda j: (0, j))],
        out_specs=pl.BlockSpec((rows, tn), lambda j: (0, j)),
        out_shape=jax.ShapeDtypeStruct((rows, n), F32),
        compiler_params=_cparams(("arbitrary",)),
        name="modulation",
    )(c_all, w_mod, b_mod)


def _adaln(x, g_ref, sc_ref, sh_ref):
    ms = jnp.mean(x * x, axis=-1, keepdims=True)
    xn = x * lax.rsqrt(ms + EPS)
    return xn * (g_ref[...] * (1.0 + sc_ref[...])) + sh_ref[...]


def _store_padded_heads(val, lo_ref_cols, hi_ref_cols, kv_ref):
    lane = lax.broadcasted_iota(jnp.int32, (1, LANES), 1)
    is_lo = lane < HEAD_DIM
    for s in range(2):
        slab = val[:, s * LANES:(s + 1) * LANES]
        swapped = pltpu.roll(slab, HEAD_DIM, 1)
        h_even, h_odd = 2 * s, 2 * s + 1
        kv_ref[:, lo_ref_cols + h_even * LANES: lo_ref_cols + (h_even + 1) * LANES] = (
            jnp.where(is_lo, slab, 0.0).astype(BF16))
        kv_ref[:, lo_ref_cols + h_odd * LANES: lo_ref_cols + (h_odd + 1) * LANES] = (
            jnp.where(is_lo, swapped, 0.0).astype(BF16))
        kv_ref[:, hi_ref_cols + h_even * LANES: hi_ref_cols + (h_even + 1) * LANES] = (
            jnp.where(is_lo, 0.0, swapped).astype(BF16))
        kv_ref[:, hi_ref_cols + h_odd * LANES: hi_ref_cols + (h_odd + 1) * LANES] = (
            jnp.where(is_lo, 0.0, slab).astype(BF16))


def _rope_slab(slab, cos, sin_signed, first_half):
    fwd = pltpu.roll(slab, LANES - ROPE_FREQS, 1)
    bwd = pltpu.roll(slab, ROPE_FREQS, 1)
    return slab * cos + jnp.where(first_half, fwd, bwd) * sin_signed


def _inproj_kernel(latent, *refs):
    if latent:
        (x_ref, sc_ref, sh_ref, g_ref, w_ref, dtb_ref, cos_ref, sin_ref,
         q_ref, kv_ref, z_ref, xbc_ref, dt_ref, ga_ref, gb_ref) = refs
    else:
        (x_ref, sc_ref, sh_ref, g_ref, w_ref, dtb_ref, kv_ref, xbc_ref, dt_ref) = refs

    h = _adaln(x_ref[...], g_ref, sc_ref, sh_ref).astype(BF16)

    def proj(off, width):
        return _dot(h, w_ref[:, off:off + width])

    k = proj(OFF_K, KV_W)
    if latent:
        cos = cos_ref[...]
        sin = sin_ref[...]
        lane = lax.broadcasted_iota(jnp.int32, (1, LANES), 1)
        first_half = (lane % (2 * ROPE_FREQS)) < ROPE_FREQS
        k = jnp.concatenate(
            [_rope_slab(k[:, s * LANES:(s + 1) * LANES], cos, sin, first_half) for s in range(2)], axis=1)
        scale = HEAD_DIM ** -0.5
        for s in range(ATT_W // LANES):
            qs = proj(OFF_Q + s * LANES, LANES)
            q_ref[:, s * LANES:(s + 1) * LANES] = (
                _rope_slab(qs, cos, sin, first_half) * scale).astype(BF16)
    _store_padded_heads(k, 0, N_KV_HEADS * LANES, kv_ref)
    _store_padded_heads(proj(OFF_V, KV_W), 2 * N_KV_HEADS * LANES, 3 * N_KV_HEADS * LANES, kv_ref)

    piece = 1024
    for c0 in range(0, CONV_CH, piece):
        xbc_ref[:, c0:c0 + piece] = proj(OFF_XBC + c0, piece).astype(BF16)

    dt_raw = proj(OFF_DT, DT_W) + dtb_ref[...]
    dt_ref[...] = jnp.maximum(dt_raw, 0.0) + jnp.log1p(jnp.exp(-jnp.abs(dt_raw)))

    if latent:
        for c0 in range(0, D_INNER, piece):
            z_ref[:, c0:c0 + piece] = _silu(proj(OFF_Z + c0, piece)).astype(BF16)
        ga_ref[...] = _sigmoid(proj(OFF_GA, D_MODEL)).astype(BF16)
        gb_ref[...] = _sigmoid(proj(OFF_GB, D_MODEL)).astype(BF16)


def _inproj(latent, x2, sc, sh, g, w_main, dt_bias, cos, sin, seq_len):
    t = x2.shape[0]
    tm = min(ROW_TILE, seq_len)
    per_seq = seq_len // tm
    d = D_MODEL
    if sc.shape[0] == 1:
        mod_spec = pl.BlockSpec((None, 1, d), lambda i: (0, 0, 0))
    else:
        mod_spec = pl.BlockSpec((None, 1, d), lambda i: (i // per_seq, 0, 0))
    row = lambda w: pl.BlockSpec((tm, w), lambda i: (i, 0))
    in_specs = [row(d), mod_spec, mod_spec, _const_spec((1, d)), _const_spec((d, N_PROJ)),
                _const_spec((1, DT_W))]
    args = [x2, sc, sh, g, w_main, dt_bias]
    sds = lambda w, dt: jax.ShapeDtypeStruct((t, w), dt)
    if latent:
        pos_spec = pl.BlockSpec((tm, LANES), lambda i: (i % per_seq, 0))
        in_specs += [pos_spec, pos_spec]
        args += [cos, sin]
        out_shape = [sds(ATT_W, BF16), sds(KV_CAT, BF16), sds(D_INNER, BF16), sds(CONV_CH, BF16),
                     sds(DT_W, F32), sds(d, BF16), sds(d, BF16)]
        out_specs = [row(ATT_W), row(KV_CAT), row(D_INNER), row(CONV_CH), row(DT_W), row(d), row(d)]
    else:
        out_shape = [sds(KV_CAT, BF16), sds(CONV_CH, BF16), sds(DT_W, F32)]
        out_specs = [row(KV_CAT), row(CONV_CH), row(DT_W)]
    return pl.pallas_call(
        functools.partial(_inproj_kernel, latent),
        grid=(t // tm,),
        in_specs=in_specs,
        out_specs=out_specs,
        out_shape=out_shape,
        compiler_params=_cparams(("parallel",)),
        name="inproj_latent" if latent else "inproj_context",
    )(*args)


def _conv_silu(buf_ref, rows, w_ref, b_ref, out_ref):
    piece = 512
    base = SUBLANES - CONV_K // 2
    for c0 in range(0, CONV_CH, piece):
        acc = b_ref[:, c0:c0 + piece] + buf_ref[pl.ds(base, rows), c0:c0 + piece] * w_ref[0:1, c0:c0 + piece]
        for tap in range(1, CONV_K):
            acc = acc + buf_ref[pl.ds(base + tap, rows), c0:c0 + piece] * w_ref[tap:tap + 1, c0:c0 + piece]
        out_ref[:, c0:c0 + piece] = _silu(acc).astype(out_ref.dtype)


def _tri():
    r = lax.broadcasted_iota(jnp.int32, (CHUNK, CHUNK), 0)
    c = lax.broadcasted_iota(jnp.int32, (CHUNK, CHUNK), 1)
    return r >= c, r <= c


def _cumsum_rows(adt, tri_bf16):
    x3 = jnp.concatenate(_split3(adt), axis=1)
    c = _dot(tri_bf16, x3)
    return c[:, :LANES] + c[:, LANES:2 * LANES] + c[:, 2 * LANES:]


def _cumsum_cols(adt_t, tri_bf16):
    x3 = jnp.concatenate(_split3(adt_t), axis=0)
    c = _dot(x3, tri_bf16)
    return c[:CHUNK] + c[CHUNK:2 * CHUNK] + c[2 * CHUNK:]


def _expand(v, e2_ref):
    hi, mid = _split2(v)
    return _dot(jnp.concatenate([hi, mid], axis=1), e2_ref[...])


def _state_update(state_ref, b_chunk, xs_chunk, w_exp, tot_exp):
    for g in range(N_SSM_GROUPS):
        cols = slice(g * GROUP_W, (g + 1) * GROUP_W)
        xw = (xs_chunk[:, cols].astype(F32) * w_exp[:, cols]).astype(BF16)
        upd = _dot_tn(b_chunk[:, g * D_STATE:(g + 1) * D_STATE], xw)
        state_ref[g] = state_ref[g] * tot_exp[:, cols] + upd


def _neg_a(alog_ref):
    return -jnp.exp(alog_ref[...])


def _fwd_state_step(state_ref, xbc_ref, r0, dt, a_row, lower_bf16, ef2_ref):
    acs = _cumsum_rows(dt * a_row, lower_bf16)
    last = acs[CHUNK - 1:CHUNK, :]
    w = jnp.exp(last - acs) * dt
    w_exp = _expand(w, ef2_ref)
    tot_exp = _expand(jnp.broadcast_to(jnp.exp(last), (SUBLANES, LANES)), ef2_ref)[0:1, :]
    rows = pl.ds(r0, CHUNK)
    _state_update(state_ref, xbc_ref[rows, D_INNER:D_INNER + BC_W], xbc_ref[rows, 0:D_INNER], w_exp, tot_exp)


def _bwd_state_step(state_ref, xbc_ref, r0, dt, a_row, upper_bf16, eb2_ref):
    s = _cumsum_rows(dt * a_row, upper_bf16)
    first = s[0:1, :]
    w = jnp.exp(first - s) * dt
    w_exp = _expand(w, eb2_ref)
    tot_exp = _expand(jnp.broadcast_to(jnp.exp(first), (SUBLANES, LANES)), eb2_ref)[0:1, :]
    rows = pl.ds(r0, CHUNK)
    _state_update(state_ref, xbc_ref[rows, D_INNER:D_INNER + BC_W], xbc_ref[rows, 0:D_INNER], w_exp, tot_exp)


def _ctx_state_kernel(n_chunks, xbc_ref, dt_ref, cw_ref, cb_ref, alog_ref, ef2_ref, eb2_ref,
                      sf_ref, sb_ref, buf_ref, act_ref):
    rows = n_chunks * CHUNK
    zeros = jnp.zeros((SUBLANES, CONV_CH), F32)
    buf_ref[0:SUBLANES, :] = zeros
    buf_ref[pl.ds(SUBLANES, rows), :] = xbc_ref[...].astype(F32)
    buf_ref[pl.ds(SUBLANES + rows, SUBLANES), :] = zeros
    _conv_silu(buf_ref, rows, cw_ref, cb_ref, act_ref)

    a_row = _neg_a(alog_ref)
    lower, upper = _tri()
    lower_bf16 = lower.astype(BF16)
    upper_bf16 = upper.astype(BF16)
    sf_ref[...] = jnp.zeros(sf_ref.shape, F32)
    sb_ref[...] = jnp.zeros(sb_ref.shape, F32)
    for c in range(n_chunks):
        _fwd_state_step(sf_ref, act_ref, c * CHUNK, dt_ref[pl.ds(c * CHUNK, CHUNK), :], a_row, lower_bf16, ef2_ref)
    for c in reversed(range(n_chunks)):
        _bwd_state_step(sb_ref, act_ref, c * CHUNK, dt_ref[pl.ds(c * CHUNK, CHUNK), :], a_row, upper_bf16, eb2_ref)


def _ctx_states(xbc_c, dt_c, conv_w8, conv_b, alog, ef2, eb2, batch, ctx_len):
    n_chunks = ctx_len // CHUNK
    state = jax.ShapeDtypeStruct((batch, N_SSM_GROUPS, D_STATE, GROUP_W), F32)
    st_spec = pl.BlockSpec((None, N_SSM_GROUPS, D_STATE, GROUP_W), lambda b: (b, 0, 0, 0))
    return pl.pallas_call(
        functools.partial(_ctx_state_kernel, n_chunks),
        grid=(batch,),
        in_specs=[pl.BlockSpec((ctx_len, CONV_CH), lambda b: (b, 0)),
                  pl.BlockSpec((ctx_len, DT_W), lambda b: (b, 0)),
                  _const_spec((SUBLANES, CONV_CH)), _const_spec((1, CONV_CH)), _const_spec((1, LANES)),
                  _const_spec(ef2.shape), _const_spec(eb2.shape)],
        out_specs=[st_spec, st_spec],
        out_shape=[state, state],
        scratch_shapes=[pltpu.VMEM((ctx_len + 2 * SUBLANES, CONV_CH), F32),
                        pltpu.VMEM((ctx_len, CONV_CH), BF16)],
        compiler_params=_cparams(("parallel",)),
        name="ssd_context_states",
    )(xbc_c, dt_c, conv_w8, conv_b, alog, ef2, eb2)


def _bwd_sweep_kernel(n_blk, cur_ref, prev_ref, next_ref, dt_ref, cw_ref, cb_ref, alog_ref, eb2_ref, sb0_ref,
                      act_ref, sin_ref, buf_ref, state_ref):
    j = pl.program_id(1)
    blk = n_blk - 1 - j
    rows = SSD_ROWS

    @pl.when(j == 0)
    def _():
        state_ref[...] = sb0_ref[...]

    prev = prev_ref[...].astype(F32)[SUBLANES:BF16_ROWS, :]
    nxt = next_ref[...].astype(F32)[0:SUBLANES, :]
    buf_ref[0:SUBLANES, :] = jnp.where(blk > 0, prev, 0.0)
    buf_ref[pl.ds(SUBLANES, rows), :] = cur_ref[...].astype(F32)
    buf_ref[pl.ds(SUBLANES + rows, SUBLANES), :] = jnp.where(blk < n_blk - 1, nxt, 0.0)
    _conv_silu(buf_ref, rows, cw_ref, cb_ref, act_ref)

    a_row = _neg_a(alog_ref)
    _, upper = _tri()
    upper_bf16 = upper.astype(BF16)
    for c in reversed(range(rows // CHUNK)):
        sin_ref[c] = state_ref[...].astype(BF16)
        _bwd_state_step(state_ref, act_ref, c * CHUNK, dt_ref[pl.ds(c * CHUNK, CHUNK), :], a_row, upper_bf16,
                        eb2_ref)


def _bwd_sweep(xbc, dt, conv_w8, conv_b, alog, eb2, sb0, batch, seq_len):
    t = xbc.shape[0]
    rows = SSD_ROWS
    n_blk = seq_len // rows
    cpb = rows // CHUNK
    halo_per_blk = rows // BF16_ROWS
    n_halo = t // BF16_ROWS

    def blk_of(b, j):
        return b * n_blk + (n_blk - 1 - j)

    return pl.pallas_call(
        functools.partial(_bwd_sweep_kernel, n_blk),
        grid=(batch, n_blk),
        in_specs=[pl.BlockSpec((rows, CONV_CH), lambda b, j: (blk_of(b, j), 0)),
                  pl.BlockSpec((BF16_ROWS, CONV_CH),
                               lambda b, j: (jnp.maximum(blk_of(b, j) * halo_per_blk - 1, 0), 0)),
                  pl.BlockSpec((BF16_ROWS, CONV_CH),
                               lambda b, j: (jnp.minimum((blk_of(b, j) + 1) * halo_per_blk, n_halo - 1), 0)),
                  pl.BlockSpec((rows, DT_W), lambda b, j: (blk_of(b, j), 0)),
                  _const_spec((SUBLANES, CONV_CH)), _const_spec((1, CONV_CH)), _const_spec((1, LANES)),
                  _const_spec(eb2.shape),
                  pl.BlockSpec((None, N_SSM_GROUPS, D_STATE, GROUP_W), lambda b, j: (b, 0, 0, 0))],
        out_specs=[pl.BlockSpec((rows, CONV_CH), lambda b, j: (blk_of(b, j), 0)),
                   pl.BlockSpec((cpb, N_SSM_GROUPS, D_STATE, GROUP_W), lambda b, j: (blk_of(b, j), 0, 0, 0))],
        out_shape=[jax.ShapeDtypeStruct((t, CONV_CH), BF16),
                   jax.ShapeDtypeStruct((t // CHUNK, N_SSM_GROUPS, D_STATE, GROUP_W), BF16)],
        scratch_shapes=[pltpu.VMEM((rows + 2 * SUBLANES, CONV_CH), F32),
                        pltpu.VMEM((N_SSM_GROUPS, D_STATE, GROUP_W), F32)],
        compiler_params=_cparams(("parallel", "arbitrary")),
        name="ssd_backward_sweep",
    )(xbc, xbc, xbc, dt, conv_w8, conv_b, alog, eb2, sb0)


def _fwd_sweep_kernel(act_ref, dt_ref, z_ref, sbin_ref, sf0_ref, alog_ref, ef2_ref, eb2_ref, dskip_ref, ng_ref,
                      y_ref, state_ref, ybuf_ref):
    j = pl.program_id(1)

    @pl.when(j == 0)
    def _():
        state_ref[...] = sf0_ref[...]

    a_row = _neg_a(alog_ref)
    lower, upper = _tri()
    lower_bf16 = lower.astype(BF16)
    upper_bf16 = upper.astype(BF16)
    lane = lax.broadcasted_iota(jnp.int32, (1, LANES), 1)
    lane_lo = lane < SSM_HEAD_DIM
    heads_per_group = N_SSM_HEADS // N_SSM_GROUPS

    for c in range(SSD_ROWS // CHUNK):
        rows = pl.ds(c * CHUNK, CHUNK)
        dt = dt_ref[rows, :]
        adt = dt * a_row
        acs_f = _cumsum_rows(adt, lower_bf16)
        acs_b = _cumsum_rows(adt, upper_bf16)
        adt_t = adt.T
        dt_t = dt.T
        acst_f = _cumsum_cols(adt_t, upper_bf16)
        acst_b = _cumsum_cols(adt_t, lower_bf16)

        b_chunk = act_ref[rows, D_INNER:D_INNER + BC_W]
        c_chunk = act_ref[rows, D_INNER + BC_W:CONV_CH]

        for g in range(N_SSM_GROUPS):
            cb = _dot_nt(c_chunk[:, g * D_STATE:(g + 1) * D_STATE], b_chunk[:, g * D_STATE:(g + 1) * D_STATE])
            for pair in range(g * heads_per_group // 2, (g + 1) * heads_per_group // 2):
                xs_pair = act_ref[rows, pair * LANES:(pair + 1) * LANES]
                outs = []
                for h in (2 * pair, 2 * pair + 1):
                    hb = N_SSM_HEADS + h
                    seg_f = acs_f[:, h:h + 1] - acst_f[h:h + 1, :]
                    seg_b = acs_b[:, hb:hb + 1] - acst_b[hb:hb + 1, :]
                    dec = (jnp.where(lower, jnp.exp(seg_f), 0.0) * dt_t[h:h + 1, :]
                           + jnp.where(upper, jnp.exp(seg_b), 0.0) * dt_t[hb:hb + 1, :])
                    outs.append(_dot((cb * dec).astype(BF16), xs_pair))
                ybuf_ref[:, pair * LANES:(pair + 1) * LANES] = jnp.where(lane_lo, outs[0], outs[1])

        ef_exp = _expand(jnp.exp(acs_f), ef2_ref)
        eb_exp = _expand(jnp.exp(acs_b), eb2_ref)
        for g in range(N_SSM_GROUPS):
            cols = slice(g * GROUP_W, (g + 1) * GROUP_W)
            c_g = c_chunk[:, g * D_STATE:(g + 1) * D_STATE]
            y = (ybuf_ref[:, cols]
                 + _dot(c_g, state_ref[g].astype(BF16)) * ef_exp[:, cols]
                 + _dot(c_g, sbin_ref[c, g]) * eb_exp[:, cols])
            y = y + dskip_ref[:, cols] * act_ref[rows, cols].astype(F32)
            y = y * z_ref[rows, cols].astype(F32)
            ms = jnp.mean(y * y, axis=-1, keepdims=True)
            y_ref[rows, cols] = (y * lax.rsqrt(ms + EPS) * ng_ref[:, cols]).astype(BF16)

        last = acs_f[CHUNK - 1:CHUNK, :]
        w_exp = _expand(jnp.exp(last - acs_f) * dt, ef2_ref)
        _state_update(state_ref, b_chunk, act_ref[rows, 0:D_INNER], w_exp, ef_exp[CHUNK - 1:CHUNK, :])


def _fwd_sweep(act, dt, z, sb_in, sf0, alog, ef2, eb2, dskip, norm_g, batch, seq_len):
    t = act.shape[0]
    rows = SSD_ROWS
    n_blk = seq_len // rows
    cpb = rows // CHUNK
    blk = lambda b, j: b * n_blk + j
    return pl.pallas_call(
        _fwd_sweep_kernel,
        grid=(batch, n_blk),
        in_specs=[pl.BlockSpec((rows, CONV_CH), lambda b, j: (blk(b, j), 0)),
                  pl.BlockSpec((rows, DT_W), lambda b, j: (blk(b, j), 0)),
                  pl.BlockSpec((rows, D_INNER), lambda b, j: (blk(b, j), 0)),
                  pl.BlockSpec((cpb, N_SSM_GROUPS, D_STATE, GROUP_W), lambda b, j: (blk(b, j), 0, 0, 0)),
                  pl.BlockSpec((None, N_SSM_GROUPS, D_STATE, GROUP_W), lambda b, j: (b, 0, 0, 0)),
                  _const_spec((1, LANES)), _const_spec(ef2.shape), _const_spec(eb2.shape),
                  _const_spec((1, D_INNER)), _const_spec((1, D_INNER))],
        out_specs=pl.BlockSpec((rows, D_INNER), lambda b, j: (blk(b, j), 0)),
        out_shape=jax.ShapeDtypeStruct((t, D_INNER), BF16),
        scratch_shapes=[pltpu.VMEM((N_SSM_GROUPS, D_STATE, GROUP_W), F32),
                        pltpu.VMEM((CHUNK, D_INNER), F32)],
        compiler_params=_cparams(("parallel", "arbitrary")),
        name="ssd_forward_sweep",
    )(act, dt, z, sb_in, sf0, alog, ef2, eb2, dskip, norm_g)


def _attn_kernel(n_blk, sink_ref, q_ref, kvp_ref, kvc_ref, kvn_ref, kvx_ref, o_ref):
    i = pl.program_id(1)
    r = lax.broadcasted_iota(jnp.int32, (ATT_BLOCK, ATT_BLOCK), 0)
    cidx = lax.broadcasted_iota(jnp.int32, (ATT_BLOCK, ATT_BLOCK), 1)
    neg = jnp.float32(-1e30)
    bias_prev = jnp.where((cidx >= r) & (i > 0), 0.0, neg)
    bias_next = jnp.where((cidx <= r) & (i < n_blk - 1), 0.0, neg)
    group = N_Q_HEADS // N_KV_HEADS
    v_off = 2 * N_KV_HEADS * LANES
    hi_off = N_KV_HEADS * LANES

    for kvh in range(N_KV_HEADS):
        for pr in range(group // 2):
            slab = kvh * (group // 2) + pr
            q = q_ref[:, slab * LANES:(slab + 1) * LANES]
            acc = None
            for half in range(2):
                head = 2 * slab + half
                kcol = half * hi_off + kvh * LANES
                vcol = v_off + kcol
                kc = slice(kcol, kcol + LANES)
                vc = slice(vcol, vcol + LANES)
                s_p = _dot_nt(q, kvp_ref[:, kc]) + bias_prev
                s_c = _dot_nt(q, kvc_ref[:, kc])
                s_n = _dot_nt(q, kvn_ref[:, kc]) + bias_next
                s_x = _dot_nt(q, kvx_ref[:, kc])
                sink = sink_ref[head]
                m = jnp.maximum(jnp.maximum(jnp.max(s_p, axis=-1, keepdims=True), jnp.max(s_c, axis=-1, keepdims=True)),
                                jnp.maximum(jnp.max(s_n, axis=-1, keepdims=True), jnp.max(s_x, axis=-1, keepdims=True)))
                m = jnp.maximum(m, sink)
                e_p = jnp.exp(s_p - m)
                e_c = jnp.exp(s_c - m)
                e_n = jnp.exp(s_n - m)
                e_x = jnp.exp(s_x - m)
                denom = (jnp.sum(e_p, axis=-1, keepdims=True) + jnp.sum(e_c, axis=-1, keepdims=True)
                         + jnp.sum(e_n, axis=-1, keepdims=True) + jnp.sum(e_x, axis=-1, keepdims=True)
                         + jnp.exp(sink - m))
                pv = (_dot(e_p.astype(BF16), kvp_ref[:, vc]) + _dot(e_c.astype(BF16), kvc_ref[:, vc])
                      + _dot(e_n.astype(BF16), kvn_ref[:, vc]) + _dot(e_x.astype(BF16), kvx_ref[:, vc]))
                pv = pv * (1.0 / denom)
                acc = pv if acc is None else acc + pv
            o_ref[:, slab * LANES:(slab + 1) * LANES] = acc.astype(BF16)


def _attention(q, kv, kv_ctx, sink, batch, seq_len, ctx_len):
    t = q.shape[0]
    n_blk = seq_len // ATT_BLOCK
    blk = lambda b, i: b * n_blk + i
    return pl.pallas_call(
        functools.partial(_attn_kernel, n_blk),
        grid=(batch, n_blk),
        in_specs=[pl.BlockSpec(memory_space=pltpu.SMEM),
                  pl.BlockSpec((ATT_BLOCK, ATT_W), lambda b, i: (blk(b, i), 0)),
                  pl.BlockSpec((ATT_BLOCK, KV_CAT), lambda b, i: (b * n_blk + jnp.maximum(i - 1, 0), 0)),
                  pl.BlockSpec((ATT_BLOCK, KV_CAT), lambda b, i: (blk(b, i), 0)),
                  pl.BlockSpec((ATT_BLOCK, KV_CAT), lambda b, i: (b * n_blk + jnp.minimum(i + 1, n_blk - 1), 0)),
                  pl.BlockSpec((ctx_len, KV_CAT), lambda b, i: (b, 0))],
        out_specs=pl.BlockSpec((ATT_BLOCK, ATT_W), lambda b, i: (blk(b, i), 0)),
        out_shape=jax.ShapeDtypeStruct((t, ATT_W), BF16),
        compiler_params=_cparams(("parallel", "parallel")),
        name="window_attention",
    )(sink, q, kv, kv, kv, kv_ctx)


def _merge_kernel(x_ref, ya_ref, ys_ref, ga_ref, gb_ref, g1_ref, woa_ref, wob_ref, wout_ref, o_ref):
    a = _dot(ya_ref[...], woa_ref[...])
    b = _dot(ys_ref[...], wob_ref[...])
    merged = (ga_ref[...].astype(F32) * a + gb_ref[...].astype(F32) * b).astype(BF16)
    o_ref[...] = x_ref[...] + g1_ref[...] * _dot(merged, wout_ref[...])


def _merge(x2, y_att, y_ssm, ga, gb, g1, w_oa, w_ob, w_out, seq_len):
    t, d = x2.shape
    tm = ROW_TILE
    per_seq = seq_len // tm
    row = lambda w: pl.BlockSpec((tm, w), lambda i: (i, 0))
    return pl.pallas_call(
        _merge_kernel,
        grid=(t // tm,),
        in_specs=[row(d), row(ATT_W), row(D_INNER), row(d), row(d),
                  pl.BlockSpec((None, 1, d), lambda i: (i // per_seq, 0, 0)),
                  _const_spec(w_oa.shape), _const_spec(w_ob.shape), _const_spec(w_out.shape)],
        out_specs=row(d),
        out_shape=jax.ShapeDtypeStruct((t, d), F32),
        compiler_params=_cparams(("parallel",)),
        name="merge_outproj",
    )(x2, y_att, y_ssm, ga, gb, g1, w_oa, w_ob, w_out)


FF_SPLIT = 1280


def _ffn_kernel(x_ref, sc_ref, sh_ref, g2_ref, ng_ref, fg_ref, win_ref, wout_ref, o_ref):
    x = x_ref[...]
    h = _adaln(x, ng_ref, sc_ref, sh_ref).astype(BF16)
    acc = None
    for c0, c1 in ((0, FF_SPLIT), (FF_SPLIT, D_FF)):
        gate = _dot(h, win_ref[:, c0:c1])
        up = _dot(h, win_ref[:, D_FF + c0:D_FF + c1])
        part = _dot((_silu(gate) * up).astype(BF16), wout_ref[c0:c1, :])
        acc = part if acc is None else acc + part
    x2 = x + g2_ref[...] * acc
    ms = jnp.mean(x2 * x2, axis=-1, keepdims=True)
    o_ref[...] = x2 * lax.rsqrt(ms + EPS) * fg_ref[...]


def _ffn(x1, sc2, sh2, g2, norm_g, final_g, w_in, w_out, seq_len):
    t, d = x1.shape
    tm = ROW_TILE
    per_seq = seq_len // tm
    row = pl.BlockSpec((tm, d), lambda i: (i, 0))
    mod_spec = pl.BlockSpec((None, 1, d), lambda i: (i // per_seq, 0, 0))
    return pl.pallas_call(
        _ffn_kernel,
        grid=(t // tm,),
        in_specs=[row, mod_spec, mod_spec, mod_spec, _const_spec((1, d)), _const_spec((1, d)),
                  _const_spec(w_in.shape), _const_spec(w_out.shape)],
        out_specs=row,
        out_shape=jax.ShapeDtypeStruct((t, d), F32),
        compiler_params=_cparams(("parallel",)),
        name="swiglu_final_norm",
    )(x1, sc2, sh2, g2, norm_g, final_g, w_in, w_out)


def _rope_tables(seq_len):
    rows = seq_len // GRID_W
    row = jnp.repeat(jnp.arange(rows), GRID_W).astype(F32)
    col = jnp.tile(jnp.arange(GRID_W), rows).astype(F32)
    inv = ROPE_BASE ** (-jnp.arange(ROPE_FREQS, dtype=F32) / ROPE_FREQS)
    ang_r = row[:, None] * inv
    ang_c = col[:, None] * inv
    ang = jnp.concatenate([ang_r, ang_r, ang_c, ang_c] * 2, axis=-1)
    sign = np.where((np.arange(LANES) % (2 * ROPE_FREQS)) < ROPE_FREQS, -1.0, 1.0).astype(np.float32)
    return jnp.cos(ang), jnp.sin(ang) * jnp.asarray(sign)[None, :]


def _expand_matrix(first_row):
    e = np.zeros((2 * LANES, D_INNER), np.float32)
    for h in range(N_SSM_HEADS):
        e[first_row + h, h * SSM_HEAD_DIM:(h + 1) * SSM_HEAD_DIM] = 1.0
        e[LANES + first_row + h, h * SSM_HEAD_DIM:(h + 1) * SSM_HEAD_DIM] = 1.0
    return jnp.asarray(e, dtype=BF16)


def kernel(x, c, ctx, c_ctx, w_mod, b_mod, norm1_g, w_in, attn_sink, conv_w, conv_b, a_log_f, a_log_b,
           dt_bias_f, dt_bias_b, d_skip, ssm_norm_g, w_oa, w_ob, w_out, norm2_g, w_ffn_in, w_ffn_out, final_g):
    batch, seq_len, d = x.shape
    ctx_len = ctx.shape[1]
    assert w_mod.shape[0] == 1 and d == D_MODEL
    assert seq_len % SSD_ROWS == 0 and seq_len % ROW_TILE == 0 and seq_len % GRID_W == 0
    assert ctx_len % CHUNK == 0 and ctx_len <= ROW_TILE

    pad_rows = (-(batch + 1)) % SUBLANES
    c_all = jnp.concatenate([c, c_ctx[None, :], jnp.zeros((pad_rows, d), F32)], axis=0)
    mod = _modulation(c_all, w_mod[0], b_mod[0][None, :])
    mod_b = mod[:batch].reshape(batch, N_MOD, 1, d)
    sh1, sc1, g1, sh2, sc2, g2 = (mod_b[:, k] for k in range(N_MOD))
    mod_c = mod[batch:batch + 1].reshape(1, N_MOD, 1, d)
    sh1c, sc1c = mod_c[:, 0], mod_c[:, 1]

    w = w_in[0]
    o = np.cumsum([0, ATT_W, KV_W, KV_W, D_INNER, CONV_CH, N_SSM_HEADS, N_SSM_HEADS, D_MODEL, D_MODEL])
    seg = lambda k: w[:, o[k]:o[k + 1]]
    w_main = jnp.concatenate(
        [seg(0), seg(1), seg(2), seg(3), seg(4), seg(7), seg(8), seg(5), seg(6),
         jnp.zeros((d, DT_W - 2 * N_SSM_HEADS), F32)], axis=1).astype(BF16)
    pad_heads = jnp.zeros((DT_W - 2 * N_SSM_HEADS,), F32)
    dt_bias = jnp.concatenate([dt_bias_f[0], dt_bias_b[0], pad_heads])[None, :]
    alog = jnp.concatenate([a_log_f[0], a_log_b[0], pad_heads])[None, :]
    conv_w8 = jnp.concatenate([conv_w[0], jnp.zeros((SUBLANES - CONV_K, CONV_CH), F32)], axis=0)
    conv_b1 = conv_b[0][None, :]
    cos, sin = _rope_tables(seq_len)
    ef2 = _expand_matrix(0)
    eb2 = _expand_matrix(N_SSM_HEADS)
    g_norm1 = norm1_g[0][None, :]

    x2 = x.reshape(batch * seq_len, d)
    xc2 = ctx.reshape(batch * ctx_len, d)

    q, kv, z, xbc, dt, ga, gb = _inproj(True, x2, sc1, sh1, g_norm1, w_main, dt_bias, cos, sin, seq_len)
    kv_c, xbc_c, dt_c = _inproj(False, xc2, sc1c, sh1c, g_norm1, w_main, dt_bias, None, None, ctx_len)

    sf0, sb0 = _ctx_states(xbc_c, dt_c, conv_w8, conv_b1, alog, ef2, eb2, batch, ctx_len)
    act, sb_in = _bwd_sweep(xbc, dt, conv_w8, conv_b1, alog, eb2, sb0, batch, seq_len)
    dskip = jnp.repeat(d_skip[0], SSM_HEAD_DIM)[None, :]
    y_ssm = _fwd_sweep(act, dt, z, sb_in, sf0, alog, ef2, eb2, dskip, ssm_norm_g[0][None, :], batch, seq_len)

    y_att = _attention(q, kv, kv_c, attn_sink[0], batch, seq_len, ctx_len)

    x1 = _merge(x2, y_att, y_ssm, ga, gb, g1, w_oa[0].astype(BF16), w_ob[0].astype(BF16),
                w_out[0].astype(BF16), seq_len)
    out = _ffn(x1, sc2, sh2, g2, norm2_g[0][None, :], final_g[None, :], w_ffn_in[0].astype(BF16),
               w_ffn_out[0].astype(BF16), seq_len)
    return out.reshape(batch, seq_len, d)
```

```python
import functools

import numpy as np
import jax
import jax.numpy as jnp
from jax import lax
from jax.experimental import pallas as pl
from jax.experimental.pallas import tpu as pltpu

F32 = jnp.float32
BF16 = jnp.bfloat16

D_MODEL = 1024
GRID_W = 64
EPS = 1e-6
HEAD_DIM = 64
N_Q_HEADS = 16
N_KV_HEADS = 4
ATT_W = N_Q_HEADS * HEAD_DIM
KV_W = N_KV_HEADS * HEAD_DIM
WINDOW = 128
ROPE_BASE = 10000.0
ROPE_FREQS = HEAD_DIM // 4
D_INNER = 2 * D_MODEL
SSM_HEAD_DIM = 64
N_SSM_HEADS = D_INNER // SSM_HEAD_DIM
N_SSM_GROUPS = 4
GROUP_W = D_INNER // N_SSM_GROUPS
D_STATE = 128
BC_W = N_SSM_GROUPS * D_STATE
CONV_K = 5
CONV_CH = D_INNER + 2 * BC_W
CHUNK = 128
D_FF = -(-8 * D_MODEL // (3 * 256)) * 256
N_MOD = 6

LANES = 128
SUBLANES = 8
BF16_ROWS = 16
VMEM_LIMIT = 56 * 1024 * 1024

OFF_Q = 0
OFF_K = OFF_Q + ATT_W
OFF_V = OFF_K + KV_W
OFF_Z = OFF_V + KV_W
OFF_XBC = OFF_Z + D_INNER
OFF_GA = OFF_XBC + CONV_CH
OFF_GB = OFF_GA + D_MODEL
OFF_DT = OFF_GB + D_MODEL
DT_W = LANES
N_PROJ = OFF_DT + DT_W
KV_CAT = 2 * N_KV_HEADS * LANES
V_OFF = N_KV_HEADS * LANES
LOG2E = 1.4426950408889634

ROW_TILE = 512
SSD_ROWS = 512
ATT_BLOCK = 128


def _cparams(sem):
    return pltpu.CompilerParams(dimension_semantics=sem, vmem_limit_bytes=VMEM_LIMIT)


def _const_spec(shape):
    n = len(shape)
    return pl.BlockSpec(shape, lambda *_: (0,) * n, pipeline_mode=pl.Buffered(1))


def _dot(a, b):
    return jnp.dot(a, b, preferred_element_type=F32)


def _dot_nt(a, b):
    return lax.dot_general(a, b, (((1,), (1,)), ((), ())), preferred_element_type=F32)


def _dot_tn(a, b):
    return lax.dot_general(a, b, (((0,), (0,)), ((), ())), preferred_element_type=F32)


def _sigmoid(v):
    return 1.0 / (1.0 + jnp.exp(-v))


def _silu(v):
    return v * _sigmoid(v)


def _split2(v):
    hi = v.astype(BF16)
    mid = (v - hi.astype(F32)).astype(BF16)
    return hi, mid


def _split3(v):
    hi = v.astype(BF16)
    r = v - hi.astype(F32)
    mid = r.astype(BF16)
    lo = (r - mid.astype(F32)).astype(BF16)
    return hi, mid, lo


def _mod_kernel(c_ref, w_ref, b_ref, o_ref):
    act = _silu(c_ref[...])
    o_ref[...] = jnp.dot(act, w_ref[...], preferred_element_type=F32,
                         precision=lax.Precision.HIGHEST) + b_ref[...]


def _modulation(c_all, w_mod, b_mod):
    rows, d = c_all.shape
    n = w_mod.shape[1]
    tn = 1024
    return pl.pallas_call(
        _mod_kernel,
        grid=(n // tn,),
        in_specs=[pl.BlockSpec((rows, d), lambda j: (0, 0)),
                  pl.BlockSpec((d, tn), lambda j: (0, j)),
                  pl.BlockSpec((1, tn), lambda j: (0, j))],
        out_specs=pl.BlockSpec((rows, tn), lambda j: (0, j)),
        out_shape=jax.ShapeDtypeStruct((rows, n), F32),
        compiler_params=_cparams(("arbitrary",)),
        name="modulation",
    )(c_all, w_mod, b_mod)


def _adaln(x, g_ref, sc_ref, sh_ref):
    ms = jnp.mean(x * x, axis=-1, keepdims=True)
    xn = x * lax.rsqrt(ms + EPS)
    return xn * (g_ref[...] * (1.0 + sc_ref[...])) + sh_ref[...]


def _store_padded_heads(val, col0, kv_ref):
    lane = lax.broadcasted_iota(jnp.int32, (1, LANES), 1)
    is_lo = lane < HEAD_DIM
    for s in range(2):
        slab = val[:, s * LANES:(s + 1) * LANES]
        swapped = pltpu.roll(slab, HEAD_DIM, 1)
        for h, src in ((2 * s, slab), (2 * s + 1, swapped)):
            kv_ref[:, col0 + h * LANES: col0 + (h + 1) * LANES] = jnp.where(is_lo, src, 0.0).astype(BF16)


def _rope_slab(slab, cos, sin_signed, first_half):
    fwd = pltpu.roll(slab, LANES - ROPE_FREQS, 1)
    bwd = pltpu.roll(slab, ROPE_FREQS, 1)
    return slab * cos + jnp.where(first_half, fwd, bwd) * sin_signed


def _inproj_kernel(latent, *refs):
    if latent:
        (x_ref, sc_ref, sh_ref, g_ref, w_ref, dtb_ref, cos_ref, sin_ref,
         q_ref, kv_ref, z_ref, xbc_ref, dt_ref, ga_ref, gb_ref) = refs
    else:
        (x_ref, sc_ref, sh_ref, g_ref, w_ref, dtb_ref, kv_ref, xbc_ref, dt_ref) = refs

    h = _adaln(x_ref[...], g_ref, sc_ref, sh_ref).astype(BF16)

    def proj(off, width):
        return _dot(h, w_ref[:, off:off + width])

    k = proj(OFF_K, KV_W)
    if latent:
        cos = cos_ref[...]
        sin = sin_ref[...]
        lane = lax.broadcasted_iota(jnp.int32, (1, LANES), 1)
        first_half = (lane % (2 * ROPE_FREQS)) < ROPE_FREQS
        k = jnp.concatenate(
            [_rope_slab(k[:, s * LANES:(s + 1) * LANES], cos, sin, first_half) for s in range(2)], axis=1)
        scale = HEAD_DIM ** -0.5 * LOG2E
        q = proj(OFF_Q, ATT_W)
        for s in range(ATT_W // LANES):
            q_ref[:, s * LANES:(s + 1) * LANES] = (
                _rope_slab(q[:, s * LANES:(s + 1) * LANES], cos, sin, first_half) * scale).astype(BF16)
    _store_padded_heads(k, 0, kv_ref)
    _store_padded_heads(proj(OFF_V, KV_W), V_OFF, kv_ref)

    piece = 1024
    for c0 in range(0, CONV_CH, piece):
        xbc_ref[:, c0:c0 + piece] = proj(OFF_XBC + c0, piece).astype(BF16)

    dt_raw = proj(OFF_DT, DT_W) + dtb_ref[...]
    dt_ref[...] = jnp.maximum(dt_raw, 0.0) + jnp.log1p(jnp.exp(-jnp.abs(dt_raw)))

    if latent:
        for c0 in range(0, D_INNER, piece):
            z_ref[:, c0:c0 + piece] = _silu(proj(OFF_Z + c0, piece)).astype(BF16)
        ga_ref[...] = _sigmoid(proj(OFF_GA, D_MODEL)).astype(BF16)
        gb_ref[...] = _sigmoid(proj(OFF_GB, D_MODEL)).astype(BF16)


def _inproj(latent, x2, sc, sh, g, w_main, dt_bias, cos, sin, seq_len):
    t = x2.shape[0]
    tm = min(ROW_TILE, seq_len)
    per_seq = seq_len // tm
    d = D_MODEL
    if sc.shape[0] == 1:
        mod_spec = pl.BlockSpec((None, 1, d), lambda i: (0, 0, 0))
    else:
        mod_spec = pl.BlockSpec((None, 1, d), lambda i: (i // per_seq, 0, 0))
    row = lambda w: pl.BlockSpec((tm, w), lambda i: (i, 0))
    in_specs = [row(d), mod_spec, mod_spec, _const_spec((1, d)), _const_spec((d, N_PROJ)),
                _const_spec((1, DT_W))]
    args = [x2, sc, sh, g, w_main, dt_bias]
    sds = lambda w, dt: jax.ShapeDtypeStruct((t, w), dt)
    if latent:
        pos_spec = pl.BlockSpec((tm, LANES), lambda i: (i % per_seq, 0))
        in_specs += [pos_spec, pos_spec]
        args += [cos, sin]
        out_shape = [sds(ATT_W, BF16), sds(KV_CAT, BF16), sds(D_INNER, BF16), sds(CONV_CH, BF16),
                     sds(DT_W, F32), sds(d, BF16), sds(d, BF16)]
        out_specs = [row(ATT_W), row(KV_CAT), row(D_INNER), row(CONV_CH), row(DT_W), row(d), row(d)]
    else:
        out_shape = [sds(KV_CAT, BF16), sds(CONV_CH, BF16), sds(DT_W, F32)]
        out_specs = [row(KV_CAT), row(CONV_CH), row(DT_W)]
    return pl.pallas_call(
        functools.partial(_inproj_kernel, latent),
        grid=(t // tm,),
        in_specs=in_specs,
        out_specs=out_specs,
        out_shape=out_shape,
        compiler_params=_cparams(("parallel",)),
        name="inproj_latent" if latent else "inproj_context",
    )(*args)


def _conv_silu(buf_ref, rows, w_ref, b_ref, out_ref):
    piece = 512
    base = SUBLANES - CONV_K // 2
    for c0 in range(0, CONV_CH, piece):
        acc = b_ref[:, c0:c0 + piece] + buf_ref[pl.ds(base, rows), c0:c0 + piece] * w_ref[0:1, c0:c0 + piece]
        for tap in range(1, CONV_K):
            acc = acc + buf_ref[pl.ds(base + tap, rows), c0:c0 + piece] * w_ref[tap:tap + 1, c0:c0 + piece]
        out_ref[:, c0:c0 + piece] = _silu(acc).astype(out_ref.dtype)


def _tri():
    r = lax.broadcasted_iota(jnp.int32, (CHUNK, CHUNK), 0)
    c = lax.broadcasted_iota(jnp.int32, (CHUNK, CHUNK), 1)
    return r >= c, r <= c


def _cumsum_rows(adt, tri_bf16):
    x3 = jnp.concatenate(_split3(adt), axis=1)
    c = _dot(tri_bf16, x3)
    return c[:, :LANES] + c[:, LANES:2 * LANES] + c[:, 2 * LANES:]


def _cumsum_cols(adt_t, tri_bf16):
    x3 = jnp.concatenate(_split3(adt_t), axis=0)
    c = _dot(x3, tri_bf16)
    return c[:CHUNK] + c[CHUNK:2 * CHUNK] + c[2 * CHUNK:]


def _expand(v, e2_ref):
    hi, mid = _split2(v)
    return _dot(jnp.concatenate([hi, mid], axis=1), e2_ref[...])


def _state_update(state_ref, b_chunk, xs_chunk, w_exp, tot_exp):
    for g in range(N_SSM_GROUPS):
        cols = slice(g * GROUP_W, (g + 1) * GROUP_W)
        xw = (xs_chunk[:, cols].astype(F32) * w_exp[:, cols]).astype(BF16)
        upd = _dot_tn(b_chunk[:, g * D_STATE:(g + 1) * D_STATE], xw)
        state_ref[g] = state_ref[g] * tot_exp[:, cols] + upd


def _neg_a_log2(alog_ref):
    return -jnp.exp(alog_ref[...]) * LOG2E


def _fwd_state_step(state_ref, xbc_ref, r0, dt, a_row, lower_bf16, ef2_ref):
    acs = _cumsum_rows(dt * a_row, lower_bf16)
    last = acs[CHUNK - 1:CHUNK, :]
    w = jnp.exp2(last - acs) * dt
    w_exp = _expand(w, ef2_ref)
    tot_exp = _expand(jnp.broadcast_to(jnp.exp2(last), (SUBLANES, LANES)), ef2_ref)[0:1, :]
    rows = pl.ds(r0, CHUNK)
    _state_update(state_ref, xbc_ref[rows, D_INNER:D_INNER + BC_W], xbc_ref[rows, 0:D_INNER], w_exp, tot_exp)


def _bwd_state_step(state_ref, xbc_ref, r0, dt, a_row, upper_bf16, eb2_ref):
    s = _cumsum_rows(dt * a_row, upper_bf16)
    first = s[0:1, :]
    w = jnp.exp2(first - s) * dt
    w_exp = _expand(w, eb2_ref)
    tot_exp = _expand(jnp.broadcast_to(jnp.exp2(first), (SUBLANES, LANES)), eb2_ref)[0:1, :]
    rows = pl.ds(r0, CHUNK)
    _state_update(state_ref, xbc_ref[rows, D_INNER:D_INNER + BC_W], xbc_ref[rows, 0:D_INNER], w_exp, tot_exp)


def _ctx_state_kernel(n_chunks, xbc_ref, dt_ref, cw_ref, cb_ref, alog_ref, ef2_ref, eb2_ref,
                      sf_ref, sb_ref, buf_ref, act_ref):
    rows = n_chunks * CHUNK
    zeros = jnp.zeros((SUBLANES, CONV_CH), F32)
    buf_ref[0:SUBLANES, :] = zeros
    buf_ref[pl.ds(SUBLANES, rows), :] = xbc_ref[...].astype(F32)
    buf_ref[pl.ds(SUBLANES + rows, SUBLANES), :] = zeros
    _conv_silu(buf_ref, rows, cw_ref, cb_ref, act_ref)

    a_row = _neg_a_log2(alog_ref)
    lower, upper = _tri()
    lower_bf16 = lower.astype(BF16)
    upper_bf16 = upper.astype(BF16)
    sf_ref[...] = jnp.zeros(sf_ref.shape, F32)
    sb_ref[...] = jnp.zeros(sb_ref.shape, F32)
    for c in range(n_chunks):
        _fwd_state_step(sf_ref, act_ref, c * CHUNK, dt_ref[pl.ds(c * CHUNK, CHUNK), :], a_row, lower_bf16, ef2_ref)
    for c in reversed(range(n_chunks)):
        _bwd_state_step(sb_ref, act_ref, c * CHUNK, dt_ref[pl.ds(c * CHUNK, CHUNK), :], a_row, upper_bf16, eb2_ref)


def _ctx_states(xbc_c, dt_c, conv_w8, conv_b, alog, ef2, eb2, batch, ctx_len):
    n_chunks = ctx_len // CHUNK
    state = jax.ShapeDtypeStruct((batch, N_SSM_GROUPS, D_STATE, GROUP_W), F32)
    st_spec = pl.BlockSpec((None, N_SSM_GROUPS, D_STATE, GROUP_W), lambda b: (b, 0, 0, 0))
    return pl.pallas_call(
        functools.partial(_ctx_state_kernel, n_chunks),
        grid=(batch,),
        in_specs=[pl.BlockSpec((ctx_len, CONV_CH), lambda b: (b, 0)),
                  pl.BlockSpec((ctx_len, DT_W), lambda b: (b, 0)),
                  _const_spec((SUBLANES, CONV_CH)), _const_spec((1, CONV_CH)), _const_spec((1, LANES)),
                  _const_spec(ef2.shape), _const_spec(eb2.shape)],
        out_specs=[st_spec, st_spec],
        out_shape=[state, state],
        scratch_shapes=[pltpu.VMEM((ctx_len + 2 * SUBLANES, CONV_CH), F32),
                        pltpu.VMEM((ctx_len, CONV_CH), BF16)],
        compiler_params=_cparams(("parallel",)),
        name="ssd_context_states",
    )(xbc_c, dt_c, conv_w8, conv_b, alog, ef2, eb2)


def _bwd_sweep_kernel(n_blk, cur_ref, prev_ref, next_ref, dt_ref, cw_ref, cb_ref, alog_ref, eb2_ref, sb0_ref,
                      act_ref, sin_ref, buf_ref, state_ref):
    j = pl.program_id(1)
    blk = n_blk - 1 - j
    rows = SSD_ROWS

    @pl.when(j == 0)
    def _():
        state_ref[...] = sb0_ref[...]

    prev = prev_ref[...].astype(F32)[SUBLANES:BF16_ROWS, :]
    nxt = next_ref[...].astype(F32)[0:SUBLANES, :]
    buf_ref[0:SUBLANES, :] = jnp.where(blk > 0, prev, 0.0)
    buf_ref[pl.ds(SUBLANES, rows), :] = cur_ref[...].astype(F32)
    buf_ref[pl.ds(SUBLANES + rows, SUBLANES), :] = jnp.where(blk < n_blk - 1, nxt, 0.0)
    _conv_silu(buf_ref, rows, cw_ref, cb_ref, act_ref)

    a_row = _neg_a_log2(alog_ref)
    _, upper = _tri()
    upper_bf16 = upper.astype(BF16)
    for c in reversed(range(rows // CHUNK)):
        sin_ref[c] = state_ref[...].astype(BF16)
        _bwd_state_step(state_ref, act_ref, c * CHUNK, dt_ref[pl.ds(c * CHUNK, CHUNK), :], a_row, upper_bf16,
                        eb2_ref)


def _bwd_sweep(xbc, dt, conv_w8, conv_b, alog, eb2, sb0, batch, seq_len):
    t = xbc.shape[0]
    rows = SSD_ROWS
    n_blk = seq_len // rows
    cpb = rows // CHUNK
    halo_per_blk = rows // BF16_ROWS
    n_halo = t // BF16_ROWS

    def blk_of(b, j):
        return b * n_blk + (n_blk - 1 - j)

    return pl.pallas_call(
        functools.partial(_bwd_sweep_kernel, n_blk),
        grid=(batch, n_blk),
        in_specs=[pl.BlockSpec((rows, CONV_CH), lambda b, j: (blk_of(b, j), 0)),
                  pl.BlockSpec((BF16_ROWS, CONV_CH),
                               lambda b, j: (jnp.maximum(blk_of(b, j) * halo_per_blk - 1, 0), 0)),
                  pl.BlockSpec((BF16_ROWS, CONV_CH),
                               lambda b, j: (jnp.minimum((blk_of(b, j) + 1) * halo_per_blk, n_halo - 1), 0)),
                  pl.BlockSpec((rows, DT_W), lambda b, j: (blk_of(b, j), 0)),
                  _const_spec((SUBLANES, CONV_CH)), _const_spec((1, CONV_CH)), _const_spec((1, LANES)),
                  _const_spec(eb2.shape),
                  pl.BlockSpec((None, N_SSM_GROUPS, D_STATE, GROUP_W), lambda b, j: (b, 0, 0, 0))],
        out_specs=[pl.BlockSpec((rows, CONV_CH), lambda b, j: (blk_of(b, j), 0)),
                   pl.BlockSpec((cpb, N_SSM_GROUPS, D_STATE, GROUP_W), lambda b, j: (blk_of(b, j), 0, 0, 0))],
        out_shape=[jax.ShapeDtypeStruct((t, CONV_CH), BF16),
                   jax.ShapeDtypeStruct((t // CHUNK, N_SSM_GROUPS, D_STATE, GROUP_W), BF16)],
        scratch_shapes=[pltpu.VMEM((rows + 2 * SUBLANES, CONV_CH), F32),
                        pltpu.VMEM((N_SSM_GROUPS, D_STATE, GROUP_W), F32)],
        compiler_params=_cparams(("parallel", "arbitrary")),
        name="ssd_backward_sweep",
    )(xbc, xbc, xbc, dt, conv_w8, conv_b, alog, eb2, sb0)


def _fwd_sweep_kernel(act_ref, dt_ref, z_ref, sbin_ref, sf0_ref, alog_ref, ef2_ref, eb2_ref, dskip_ref, ng_ref,
                      y_ref, state_ref, ybuf_ref):
    j = pl.program_id(1)

    @pl.when(j == 0)
    def _():
        state_ref[...] = sf0_ref[...]

    a_row = _neg_a_log2(alog_ref)
    lower, upper = _tri()
    lower_bf16 = lower.astype(BF16)
    upper_bf16 = upper.astype(BF16)
    diag = lower & upper
    lane = lax.broadcasted_iota(jnp.int32, (1, LANES), 1)
    lane_lo = lane < SSM_HEAD_DIM
    heads_per_group = N_SSM_HEADS // N_SSM_GROUPS

    for c in range(SSD_ROWS // CHUNK):
        rows = pl.ds(c * CHUNK, CHUNK)
        dt = dt_ref[rows, :]
        adt = dt * a_row
        acs_f = _cumsum_rows(adt, lower_bf16)
        acs_b = _cumsum_rows(adt, upper_bf16)
        adt_t = adt.T
        dt_t = dt.T
        acst_f = _cumsum_cols(adt_t, upper_bf16)
        acst_b = _cumsum_cols(adt_t, lower_bf16)
        log2_dt_t = jnp.log2(dt_t)
        row_f = acst_f - log2_dt_t
        row_b = acst_b - log2_dt_t
        diag_log = jnp.log2(dt_t[0:N_SSM_HEADS] + dt_t[N_SSM_HEADS:2 * N_SSM_HEADS])

        b_chunk = act_ref[rows, D_INNER:D_INNER + BC_W]
        c_chunk = act_ref[rows, D_INNER + BC_W:CONV_CH]

        for g in range(N_SSM_GROUPS):
            cb = _dot_nt(c_chunk[:, g * D_STATE:(g + 1) * D_STATE], b_chunk[:, g * D_STATE:(g + 1) * D_STATE])
            for pair in range(g * heads_per_group // 2, (g + 1) * heads_per_group // 2):
                xs_pair = act_ref[rows, pair * LANES:(pair + 1) * LANES]
                outs = []
                for h in (2 * pair, 2 * pair + 1):
                    hb = N_SSM_HEADS + h
                    seg_f = acs_f[:, h:h + 1] - row_f[h:h + 1, :]
                    seg_b = acs_b[:, hb:hb + 1] - row_b[hb:hb + 1, :]
                    seg = jnp.where(diag, diag_log[h:h + 1, :], jnp.where(lower, seg_f, seg_b))
                    outs.append(_dot((cb * jnp.exp2(seg)).astype(BF16), xs_pair))
                ybuf_ref[:, pair * LANES:(pair + 1) * LANES] = jnp.where(lane_lo, outs[0], outs[1])

        ef_exp = _expand(jnp.exp2(acs_f), ef2_ref)
        eb_exp = _expand(jnp.exp2(acs_b), eb2_ref)
        for g in range(N_SSM_GROUPS):
            cols = slice(g * GROUP_W, (g + 1) * GROUP_W)
            c_g = c_chunk[:, g * D_STATE:(g + 1) * D_STATE]
            y = (ybuf_ref[:, cols]
                 + _dot(c_g, state_ref[g].astype(BF16)) * ef_exp[:, cols]
                 + _dot(c_g, sbin_ref[c, g]) * eb_exp[:, cols])
            y = y + dskip_ref[:, cols] * act_ref[rows, cols].astype(F32)
            y = y * z_ref[rows, cols].astype(F32)
            ms = jnp.mean(y * y, axis=-1, keepdims=True)
            y_ref[rows, cols] = (y * lax.rsqrt(ms + EPS) * ng_ref[:, cols]).astype(BF16)

        last = acs_f[CHUNK - 1:CHUNK, :]
        w_exp = _expand(jnp.exp2(last - acs_f) * dt, ef2_ref)
        _state_update(state_ref, b_chunk, act_ref[rows, 0:D_INNER], w_exp, ef_exp[CHUNK - 1:CHUNK, :])


def _fwd_sweep(act, dt, z, sb_in, sf0, alog, ef2, eb2, dskip, norm_g, batch, seq_len):
    t = act.shape[0]
    rows = SSD_ROWS
    n_blk = seq_len // rows
    cpb = rows // CHUNK
    blk = lambda b, j: b * n_blk + j
    return pl.pallas_call(
        _fwd_sweep_kernel,
        grid=(batch, n_blk),
        in_specs=[pl.BlockSpec((rows, CONV_CH), lambda b, j: (blk(b, j), 0)),
                  pl.BlockSpec((rows, DT_W), lambda b, j: (blk(b, j), 0)),
                  pl.BlockSpec((rows, D_INNER), lambda b, j: (blk(b, j), 0)),
                  pl.BlockSpec((cpb, N_SSM_GROUPS, D_STATE, GROUP_W), lambda b, j: (blk(b, j), 0, 0, 0)),
                  pl.BlockSpec((None, N_SSM_GROUPS, D_STATE, GROUP_W), lambda b, j: (b, 0, 0, 0)),
                  _const_spec((1, LANES)), _const_spec(ef2.shape), _const_spec(eb2.shape),
                  _const_spec((1, D_INNER)), _const_spec((1, D_INNER))],
        out_specs=pl.BlockSpec((rows, D_INNER), lambda b, j: (blk(b, j), 0)),
        out_shape=jax.ShapeDtypeStruct((t, D_INNER), BF16),
        scratch_shapes=[pltpu.VMEM((N_SSM_GROUPS, D_STATE, GROUP_W), F32),
                        pltpu.VMEM((CHUNK, D_INNER), F32)],
        compiler_params=_cparams(("parallel", "arbitrary")),
        name="ssd_forward_sweep",
    )(act, dt, z, sb_in, sf0, alog, ef2, eb2, dskip, norm_g)


def _attn_kernel(n_steps, sink_ref, q_ref, kv0_ref, kv1_ref, kv2_ref, kv3_ref, kvx_ref, o_ref):
    i = pl.program_id(1)
    group = N_Q_HEADS // N_KV_HEADS
    width = group * ATT_BLOCK
    key_j = lax.broadcasted_iota(jnp.int32, (ATT_BLOCK, width), 0)
    lane = lax.broadcasted_iota(jnp.int32, (ATT_BLOCK, width), 1)
    query_r = lane % ATT_BLOCK
    neg = jnp.float32(-1e30)
    band_prev = key_j >= query_r
    band_next = key_j <= query_r
    lane1 = lax.broadcasted_iota(jnp.int32, (1, width), 1)
    half = HEAD_DIM

    def swap_halves(t):
        return jnp.concatenate([t[half:], t[:half]], axis=0)

    for sub, kvh in [(sub, kvh) for sub in range(2) for kvh in range(N_KV_HEADS)]:
        qrows = slice(sub * ATT_BLOCK, (sub + 1) * ATT_BLOCK)
        kvp_ref, kvc_ref, kvn_ref = (kv0_ref, kv1_ref, kv2_ref) if sub == 0 else (kv1_ref, kv2_ref, kv3_ref)
        bias_prev = jnp.where(band_prev & (i > 0), 0.0, neg) if sub == 0 else jnp.where(band_prev, 0.0, neg)
        bias_next = jnp.where(band_next, 0.0, neg) if sub == 0 else jnp.where(band_next & (i < n_steps - 1), 0.0, neg)
        t0 = q_ref[qrows, (2 * kvh) * LANES:(2 * kvh + 1) * LANES].astype(F32).T.astype(BF16)
        t1 = q_ref[qrows, (2 * kvh + 1) * LANES:(2 * kvh + 2) * LANES].astype(F32).T.astype(BF16)
        q_t = jnp.concatenate([t0, swap_halves(t0), t1, swap_halves(t1)], axis=1)
        kc = slice(kvh * LANES, (kvh + 1) * LANES)
        vc = slice(V_OFF + kvh * LANES, V_OFF + (kvh + 1) * LANES)
        k_all = jnp.concatenate([kvp_ref[:, kc], kvc_ref[:, kc], kvn_ref[:, kc], kvx_ref[:, kc]], axis=0)
        v_all = jnp.concatenate([kvp_ref[:, vc], kvc_ref[:, vc], kvn_ref[:, vc], kvx_ref[:, vc]], axis=0)
        s = _dot(k_all, q_t)
        s_p = s[0:ATT_BLOCK] + bias_prev
        s_c = s[ATT_BLOCK:2 * ATT_BLOCK]
        s_n = s[2 * ATT_BLOCK:3 * ATT_BLOCK] + bias_next
        s_x = s[3 * ATT_BLOCK:]
        sink = jnp.zeros((1, width), F32)
        for g in range(group):
            sink = jnp.where(lane1 // ATT_BLOCK == g, sink_ref[kvh * group + g] * LOG2E, sink)
        red = lambda f, parts: functools.reduce(f, parts)
        m = red(jnp.maximum, [jnp.max(p, axis=0, keepdims=True) for p in (s_p, s_c, s_n, s_x)])
        m = jnp.maximum(m, sink)
        e = [jnp.exp2(p - m) for p in (s_p, s_c, s_n, s_x)]
        denom = red(jnp.add, [jnp.sum(p, axis=0, keepdims=True) for p in e]) + jnp.exp2(sink - m)
        p_all = jnp.concatenate([p.astype(BF16) for p in e], axis=0)
        v_t = v_all.T[:half]
        out_t = _dot(v_t, p_all) * (1.0 / denom)
        for sl in range(2):
            pair_t = jnp.concatenate([out_t[:, (2 * sl) * ATT_BLOCK:(2 * sl + 1) * ATT_BLOCK],
                                      out_t[:, (2 * sl + 1) * ATT_BLOCK:(2 * sl + 2) * ATT_BLOCK]], axis=0)
            o_ref[qrows, (2 * kvh + sl) * LANES:(2 * kvh + sl + 1) * LANES] = pair_t.T.astype(BF16)


def _attention(q, kv, kv_ctx, sink, batch, seq_len, ctx_len):
    t = q.shape[0]
    n_blk = seq_len // ATT_BLOCK
    n_steps = n_blk // 2
    qrows = 2 * ATT_BLOCK

    def kv_spec(offset):
        return pl.BlockSpec((ATT_BLOCK, KV_CAT),
                            lambda b, i: (b * n_blk + jnp.clip(2 * i + offset, 0, n_blk - 1), 0))

    return pl.pallas_call(
        functools.partial(_attn_kernel, n_steps),
        grid=(batch, n_steps),
        in_specs=[pl.BlockSpec(memory_space=pltpu.SMEM),
                  pl.BlockSpec((qrows, ATT_W), lambda b, i: (b * n_steps + i, 0)),
                  kv_spec(-1), kv_spec(0), kv_spec(1), kv_spec(2),
                  pl.BlockSpec((ctx_len, KV_CAT), lambda b, i: (b, 0))],
        out_specs=pl.BlockSpec((qrows, ATT_W), lambda b, i: (b * n_steps + i, 0)),
        out_shape=jax.ShapeDtypeStruct((t, ATT_W), BF16),
        compiler_params=_cparams(("parallel", "parallel")),
        name="window_attention",
    )(sink, q, kv, kv, kv, kv, kv_ctx)


FF_SPLIT = 1280


def _merge_ffn_kernel(x_ref, ya_ref, ys_ref, ga_ref, gb_ref, g1_ref, sc_ref, sh_ref, g2_ref, ng_ref, fg_ref,
                      woa_ref, wob_ref, wout_ref, win_ref, wffo_ref, o_ref):
    a = _dot(ya_ref[...], woa_ref[...])
    b = _dot(ys_ref[...], wob_ref[...])
    merged = (ga_ref[...].astype(F32) * a + gb_ref[...].astype(F32) * b).astype(BF16)
    x1 = x_ref[...] + g1_ref[...] * _dot(merged, wout_ref[...])

    h = _adaln(x1, ng_ref, sc_ref, sh_ref).astype(BF16)
    acc = None
    for c0, c1 in ((0, FF_SPLIT), (FF_SPLIT, D_FF)):
        gate = _dot(h, win_ref[:, c0:c1])
        up = _dot(h, win_ref[:, D_FF + c0:D_FF + c1])
        part = _dot((_silu(gate) * up).astype(BF16), wffo_ref[c0:c1, :])
        acc = part if acc is None else acc + part
    x2 = x1 + g2_ref[...] * acc
    ms = jnp.mean(x2 * x2, axis=-1, keepdims=True)
    o_ref[...] = x2 * lax.rsqrt(ms + EPS) * fg_ref[...]


def _merge_ffn(x2, y_att, y_ssm, ga, gb, g1, sc2, sh2, g2, norm_g, final_g, w_oa, w_ob, w_out, w_ffn_in, w_ffn_out,
               seq_len):
    t, d = x2.shape
    tm = ROW_TILE
    per_seq = seq_len // tm
    row = lambda w: pl.BlockSpec((tm, w), lambda i: (i, 0))
    mod_spec = pl.BlockSpec((None, 1, d), lambda i: (i // per_seq, 0, 0))
    return pl.pallas_call(
        _merge_ffn_kernel,
        grid=(t // tm,),
        in_specs=[row(d), row(ATT_W), row(D_INNER), row(d), row(d), mod_spec, mod_spec, mod_spec, mod_spec,
                  _const_spec((1, d)), _const_spec((1, d)),
                  _const_spec(w_oa.shape), _const_spec(w_ob.shape), _const_spec(w_out.shape),
                  _const_spec(w_ffn_in.shape), _const_spec(w_ffn_out.shape)],
        out_specs=row(d),
        out_shape=jax.ShapeDtypeStruct((t, d), F32),
        compiler_params=_cparams(("parallel",)),
        name="merge_ffn_final_norm",
    )(x2, y_att, y_ssm, ga, gb, g1, sc2, sh2, g2, norm_g, final_g, w_oa, w_ob, w_out, w_ffn_in, w_ffn_out)


def _rope_tables(seq_len):
    rows = seq_len // GRID_W
    row = jnp.repeat(jnp.arange(rows), GRID_W).astype(F32)
    col = jnp.tile(jnp.arange(GRID_W), rows).astype(F32)
    inv = ROPE_BASE ** (-jnp.arange(ROPE_FREQS, dtype=F32) / ROPE_FREQS)
    ang_r = row[:, None] * inv
    ang_c = col[:, None] * inv
    ang = jnp.concatenate([ang_r, ang_r, ang_c, ang_c] * 2, axis=-1)
    sign = np.where((np.arange(LANES) % (2 * ROPE_FREQS)) < ROPE_FREQS, -1.0, 1.0).astype(np.float32)
    return jnp.cos(ang), jnp.sin(ang) * jnp.asarray(sign)[None, :]


def _expand_matrix(first_row):
    e = np.zeros((2 * LANES, D_INNER), np.float32)
    for h in range(N_SSM_HEADS):
        e[first_row + h, h * SSM_HEAD_DIM:(h + 1) * SSM_HEAD_DIM] = 1.0
        e[LANES + first_row + h, h * SSM_HEAD_DIM:(h + 1) * SSM_HEAD_DIM] = 1.0
    return jnp.asarray(e, dtype=BF16)


def kernel(x, c, ctx, c_ctx, w_mod, b_mod, norm1_g, w_in, attn_sink, conv_w, conv_b, a_log_f, a_log_b,
           dt_bias_f, dt_bias_b, d_skip, ssm_norm_g, w_oa, w_ob, w_out, norm2_g, w_ffn_in, w_ffn_out, final_g):
    batch, seq_len, d = x.shape
    ctx_len = ctx.shape[1]
    assert w_mod.shape[0] == 1 and d == D_MODEL
    assert seq_len % SSD_ROWS == 0 and seq_len % ROW_TILE == 0 and seq_len % GRID_W == 0
    assert seq_len % (2 * ATT_BLOCK) == 0
    assert ctx_len % CHUNK == 0 and ctx_len <= ROW_TILE

    pad_rows = (-(batch + 1)) % SUBLANES
    c_all = jnp.concatenate([c, c_ctx[None, :], jnp.zeros((pad_rows, d), F32)], axis=0)
    mod = _modulation(c_all, w_mod[0], b_mod[0][None, :])
    mod_b = mod[:batch].reshape(batch, N_MOD, 1, d)
    sh1, sc1, g1, sh2, sc2, g2 = (mod_b[:, k] for k in range(N_MOD))
    mod_c = mod[batch:batch + 1].reshape(1, N_MOD, 1, d)
    sh1c, sc1c = mod_c[:, 0], mod_c[:, 1]

    w = w_in[0]
    o = np.cumsum([0, ATT_W, KV_W, KV_W, D_INNER, CONV_CH, N_SSM_HEADS, N_SSM_HEADS, D_MODEL, D_MODEL])
    seg = lambda k: w[:, o[k]:o[k + 1]]
    w_main = jnp.concatenate(
        [seg(0), seg(1), seg(2), seg(3), seg(4), seg(7), seg(8), seg(5), seg(6),
         jnp.zeros((d, DT_W - 2 * N_SSM_HEADS), F32)], axis=1).astype(BF16)
    pad_heads = jnp.zeros((DT_W - 2 * N_SSM_HEADS,), F32)
    dt_bias = jnp.concatenate([dt_bias_f[0], dt_bias_b[0], pad_heads])[None, :]
    alog = jnp.concatenate([a_log_f[0], a_log_b[0], pad_heads])[None, :]
    conv_w8 = jnp.concatenate([conv_w[0], jnp.zeros((SUBLANES - CONV_K, CONV_CH), F32)], axis=0)
    conv_b1 = conv_b[0][None, :]
    cos, sin = _rope_tables(seq_len)
    ef2 = _expand_matrix(0)
    eb2 = _expand_matrix(N_SSM_HEADS)
    g_norm1 = norm1_g[0][None, :]

    x2 = x.reshape(batch * seq_len, d)
    xc2 = ctx.reshape(batch * ctx_len, d)

    q, kv, z, xbc, dt, ga, gb = _inproj(True, x2, sc1, sh1, g_norm1, w_main, dt_bias, cos, sin, seq_len)
    kv_c, xbc_c, dt_c = _inproj(False, xc2, sc1c, sh1c, g_norm1, w_main, dt_bias, None, None, ctx_len)

    sf0, sb0 = _ctx_states(xbc_c, dt_c, conv_w8, conv_b1, alog, ef2, eb2, batch, ctx_len)
    act, sb_in = _bwd_sweep(xbc, dt, conv_w8, conv_b1, alog, eb2, sb0, batch, seq_len)
    dskip = jnp.repeat(d_skip[0], SSM_HEAD_DIM)[None, :]
    y_ssm = _fwd_sweep(act, dt, z, sb_in, sf0, alog, ef2, eb2, dskip, ssm_norm_g[0][None, :], batch, seq_len)

    y_att = _attention(q, kv, kv_c, attn_sink[0], batch, seq_len, ctx_len)

    out = _merge_ffn(x2, y_att, y_ssm, ga, gb, g1, sc2, sh2, g2, norm2_g[0][None, :], final_g[None, :],
                     w_oa[0].astype(BF16), w_ob[0].astype(BF16), w_out[0].astype(BF16),
                     w_ffn_in[0].astype(BF16), w_ffn_out[0].astype(BF16), seq_len)
    return out.reshape(batch, seq_len, d)
```

```python
import functools

import numpy as np
import jax
import jax.numpy as jnp
from jax import lax
from jax.experimental import pallas as pl
from jax.experimental.pallas import tpu as pltpu

F32 = jnp.float32
BF16 = jnp.bfloat16

D_MODEL = 1024
GRID_W = 64
EPS = 1e-6
HEAD_DIM = 64
N_Q_HEADS = 16
N_KV_HEADS = 4
ATT_W = N_Q_HEADS * HEAD_DIM
KV_W = N_KV_HEADS * HEAD_DIM
WINDOW = 128
ROPE_BASE = 10000.0
ROPE_FREQS = HEAD_DIM // 4
D_INNER = 2 * D_MODEL
SSM_HEAD_DIM = 64
N_SSM_HEADS = D_INNER // SSM_HEAD_DIM
N_SSM_GROUPS = 4
GROUP_W = D_INNER // N_SSM_GROUPS
D_STATE = 128
BC_W = N_SSM_GROUPS * D_STATE
CONV_K = 5
CONV_CH = D_INNER + 2 * BC_W
CHUNK = 128
CONV_WIN = CHUNK + 16
D_FF = -(-8 * D_MODEL // (3 * 256)) * 256
N_MOD = 6

LANES = 128
SUBLANES = 8
BF16_ROWS = 16
VMEM_LIMIT = 56 * 1024 * 1024

OFF_Q = 0
OFF_K = OFF_Q + ATT_W
OFF_V = OFF_K + KV_W
OFF_Z = OFF_V + KV_W
OFF_XBC = OFF_Z + D_INNER
OFF_GA = OFF_XBC + CONV_CH
OFF_GB = OFF_GA + D_MODEL
OFF_DT = OFF_GB + D_MODEL
DT_W = LANES
N_PROJ = OFF_DT + DT_W
KV_CAT = 2 * N_KV_HEADS * LANES
V_OFF = N_KV_HEADS * LANES
LOG2E = 1.4426950408889634

ROW_TILE = 512
SSD_ROWS = 512
ATT_BLOCK = 128


def _cparams(sem):
    return pltpu.CompilerParams(dimension_semantics=sem, vmem_limit_bytes=VMEM_LIMIT)


def _const_spec(shape):
    n = len(shape)
    return pl.BlockSpec(shape, lambda *_: (0,) * n, pipeline_mode=pl.Buffered(1))


def _dot(a, b):
    return jnp.dot(a, b, preferred_element_type=F32)


def _dot_nt(a, b):
    return lax.dot_general(a, b, (((1,), (1,)), ((), ())), preferred_element_type=F32)


def _dot_tn(a, b):
    return lax.dot_general(a, b, (((0,), (0,)), ((), ())), preferred_element_type=F32)


def _sigmoid(v):
    return 1.0 / (1.0 + jnp.exp(-v))


def _silu(v):
    return v * _sigmoid(v)


def _split3(v):
    hi = v.astype(BF16)
    r = v - hi.astype(F32)
    mid = r.astype(BF16)
    lo = (r - mid.astype(F32)).astype(BF16)
    return hi, mid, lo


def _mod_kernel(c_ref, w_ref, b_ref, o_ref):
    act = _silu(c_ref[...])
    o_ref[...] = jnp.dot(act, w_ref[...], preferred_element_type=F32,
                         precision=lax.Precision.HIGHEST) + b_ref[...]


def _modulation(c_all, w_mod, b_mod):
    rows, d = c_all.shape
    n = w_mod.shape[1]
    tn = 1024
    return pl.pallas_call(
        _mod_kernel,
        grid=(n // tn,),
        in_specs=[pl.BlockSpec((rows, d), lambda j: (0, 0)),
                  pl.BlockSpec((d, tn), lambda j: (0, j)),
                  pl.BlockSpec((1, tn), lambda j: (0, j))],
        out_specs=pl.BlockSpec((rows, tn), lambda j: (0, j)),
        out_shape=jax.ShapeDtypeStruct((rows, n), F32),
        compiler_params=_cparams(("arbitrary",)),
        name="modulation",
    )(c_all, w_mod, b_mod)


def _adaln(x, g_ref, sc_ref, sh_ref):
    ms = jnp.mean(x * x, axis=-1, keepdims=True)
    xn = x * lax.rsqrt(ms + EPS)
    return xn * (g_ref[...] * (1.0 + sc_ref[...])) + sh_ref[...]


def _store_padded_heads(val, col0, kv_ref):
    lane = lax.broadcasted_iota(jnp.int32, (1, LANES), 1)
    is_lo = lane < HEAD_DIM
    for s in range(2):
        slab = val[:, s * LANES:(s + 1) * LANES]
        swapped = pltpu.roll(slab, HEAD_DIM, 1)
        for h, src in ((2 * s, slab), (2 * s + 1, swapped)):
            kv_ref[:, col0 + h * LANES: col0 + (h + 1) * LANES] = jnp.where(is_lo, src, 0.0).astype(BF16)


def _rope_slab(slab, cos, sin_signed, first_half):
    fwd = pltpu.roll(slab, LANES - ROPE_FREQS, 1)
    bwd = pltpu.roll(slab, ROPE_FREQS, 1)
    return slab * cos + jnp.where(first_half, fwd, bwd) * sin_signed


def _inproj_kernel(latent, *refs):
    if latent:
        (x_ref, sc_ref, sh_ref, g_ref, w_ref, dtb_ref, cos_ref, sin_ref,
         q_ref, kv_ref, z_ref, xbc_ref, dt_ref, ga_ref, gb_ref) = refs
    else:
        (x_ref, sc_ref, sh_ref, g_ref, w_ref, dtb_ref, kv_ref, xbc_ref, dt_ref) = refs

    h = _adaln(x_ref[...], g_ref, sc_ref, sh_ref).astype(BF16)

    def proj(off, width):
        return _dot(h, w_ref[:, off:off + width])

    k = proj(OFF_K, KV_W)
    if latent:
        cos = cos_ref[...]
        sin = sin_ref[...]
        lane = lax.broadcasted_iota(jnp.int32, (1, LANES), 1)
        first_half = (lane % (2 * ROPE_FREQS)) < ROPE_FREQS
        k = jnp.concatenate(
            [_rope_slab(k[:, s * LANES:(s + 1) * LANES], cos, sin, first_half) for s in range(2)], axis=1)
        scale = HEAD_DIM ** -0.5 * LOG2E
        q = proj(OFF_Q, ATT_W)
        for s in range(ATT_W // LANES):
            q_ref[:, s * LANES:(s + 1) * LANES] = (
                _rope_slab(q[:, s * LANES:(s + 1) * LANES], cos, sin, first_half) * scale).astype(BF16)
    _store_padded_heads(k, 0, kv_ref)
    _store_padded_heads(proj(OFF_V, KV_W), V_OFF, kv_ref)

    piece = 1024
    for c0 in range(0, CONV_CH, piece):
        xbc_ref[:, c0:c0 + piece] = proj(OFF_XBC + c0, piece).astype(BF16)

    dt_raw = proj(OFF_DT, DT_W) + dtb_ref[...]
    dt_ref[...] = jnp.maximum(dt_raw, 0.0) + jnp.log1p(jnp.exp(-jnp.abs(dt_raw)))

    if latent:
        for c0 in range(0, D_INNER, piece):
            z_ref[:, c0:c0 + piece] = _silu(proj(OFF_Z + c0, piece)).astype(BF16)
        ga_ref[...] = _sigmoid(proj(OFF_GA, D_MODEL)).astype(BF16)
        gb_ref[...] = _sigmoid(proj(OFF_GB, D_MODEL)).astype(BF16)


def _inproj(latent, x2, sc, sh, g, w_main, dt_bias, cos, sin, seq_len):
    t = x2.shape[0]
    tm = min(ROW_TILE, seq_len)
    per_seq = seq_len // tm
    d = D_MODEL
    if sc.shape[0] == 1:
        mod_spec = pl.BlockSpec((None, 1, d), lambda i: (0, 0, 0))
    else:
        mod_spec = pl.BlockSpec((None, 1, d), lambda i: (i // per_seq, 0, 0))
    row = lambda w: pl.BlockSpec((tm, w), lambda i: (i, 0))
    in_specs = [row(d), mod_spec, mod_spec, _const_spec((1, d)), _const_spec((d, N_PROJ)),
                _const_spec((1, DT_W))]
    args = [x2, sc, sh, g, w_main, dt_bias]
    sds = lambda w, dt: jax.ShapeDtypeStruct((t, w), dt)
    if latent:
        pos_spec = pl.BlockSpec((tm, LANES), lambda i: (i % per_seq, 0))
        in_specs += [pos_spec, pos_spec]
        args += [cos, sin]
        out_shape = [sds(ATT_W, BF16), sds(KV_CAT, BF16), sds(D_INNER, BF16), sds(CONV_CH, BF16),
                     sds(DT_W, F32), sds(d, BF16), sds(d, BF16)]
        out_specs = [row(ATT_W), row(KV_CAT), row(D_INNER), row(CONV_CH), row(DT_W), row(d), row(d)]
    else:
        out_shape = [sds(KV_CAT, BF16), sds(CONV_CH, BF16), sds(DT_W, F32)]
        out_specs = [row(KV_CAT), row(CONV_CH), row(DT_W)]
    return pl.pallas_call(
        functools.partial(_inproj_kernel, latent),
        grid=(t // tm,),
        in_specs=in_specs,
        out_specs=out_specs,
        out_shape=out_shape,
        compiler_params=_cparams(("parallel",)),
        name="inproj_latent" if latent else "inproj_context",
    )(*args)


def _conv_silu(buf_ref, n_blocks, smat_ref, w_ref, b_ref, p_ref, out_ref):
    for m in range(n_blocks):
        r0 = m * CHUNK
        slot = m % 2
        for k in range(CONV_K):
            start = r0 if k <= CONV_K // 2 else r0 + BF16_ROWS
            p_ref[slot, k * CONV_WIN:(k + 1) * CONV_WIN, :] = (
                buf_ref[pl.ds(start, CONV_WIN), :] * w_ref[k:k + 1, :].astype(BF16))
        acc = _dot(smat_ref[...], p_ref[slot]) + b_ref[...]
        out_ref[pl.ds(r0, CHUNK), :] = _silu(acc).astype(out_ref.dtype)


def _tri():
    r = lax.broadcasted_iota(jnp.int32, (CHUNK, CHUNK), 0)
    c = lax.broadcasted_iota(jnp.int32, (CHUNK, CHUNK), 1)
    return r >= c, r <= c


def _cumsum_rows(adts, tri_bf16):
    x3 = jnp.concatenate([t for a in adts for t in _split3(a)], axis=1)
    c = _dot(tri_bf16, x3)
    w = 3 * LANES
    return [c[:, i * w:i * w + LANES] + c[:, i * w + LANES:i * w + 2 * LANES] + c[:, i * w + 2 * LANES:(i + 1) * w]
            for i in range(len(adts))]


def _cumsum_cols(adt_ts, tri_bf16):
    x3 = jnp.concatenate([t for a in adt_ts for t in _split3(a)], axis=0)
    c = _dot(x3, tri_bf16)
    h = 3 * CHUNK
    return [c[i * h:i * h + CHUNK] + c[i * h + CHUNK:i * h + 2 * CHUNK] + c[i * h + 2 * CHUNK:(i + 1) * h]
            for i in range(len(adt_ts))]


def _expand(v, e_ref):
    return _dot(v.astype(BF16), e_ref[...])


def _state_update(state_ref, b_chunk, xs_chunk, w_exp, tot_exp):
    for g in range(N_SSM_GROUPS):
        cols = slice(g * GROUP_W, (g + 1) * GROUP_W)
        xw = (xs_chunk[:, cols].astype(F32) * w_exp[:, cols]).astype(BF16)
        upd = _dot_tn(b_chunk[:, g * D_STATE:(g + 1) * D_STATE], xw)
        state_ref[g] = state_ref[g] * tot_exp[:, cols] + upd


def _neg_a_log2(alog_ref):
    return -jnp.exp(alog_ref[...]) * LOG2E


def _state_weights(acs, dts, e_ref, total_row):
    totals = [a[total_row:total_row + 1, :] for a in acs]
    w = jnp.concatenate([jnp.exp2(t - a) * d for t, a, d in zip(totals, acs, dts)], axis=0)
    pad = jnp.zeros((SUBLANES - len(acs), LANES), F32)
    tot = jnp.concatenate([jnp.exp2(t) for t in totals] + [pad], axis=0)
    return _expand(w, e_ref), _expand(tot, e_ref)


def _chunk_state_update(state_ref, act_ref, c, w_exp, tot_exp):
    rows = pl.ds(c * CHUNK, CHUNK)
    _state_update(state_ref, act_ref[rows, D_INNER:D_INNER + BC_W], act_ref[rows, 0:D_INNER],
                  w_exp[c * CHUNK:(c + 1) * CHUNK], tot_exp[c:c + 1])


def _ctx_state_kernel(n_chunks, xbc_ref, dt_ref, smat_ref, cw_ref, cb_ref, alog_ref, ef_ref, eb_ref,
                      sf_ref, sb_ref, buf_ref, p_ref, act_ref):
    rows = n_chunks * CHUNK
    zeros = jnp.zeros((BF16_ROWS, CONV_CH), BF16)
    buf_ref[0:BF16_ROWS, :] = zeros
    buf_ref[pl.ds(BF16_ROWS, rows), :] = xbc_ref[...]
    buf_ref[pl.ds(BF16_ROWS + rows, BF16_ROWS), :] = zeros
    _conv_silu(buf_ref, n_chunks, smat_ref, cw_ref, cb_ref, p_ref, act_ref)

    a_row = _neg_a_log2(alog_ref)
    lower, upper = _tri()
    dts = [dt_ref[pl.ds(c * CHUNK, CHUNK), :] for c in range(n_chunks)]
    adts = [d * a_row for d in dts]
    wf_exp, totf_exp = _state_weights(_cumsum_rows(adts, lower.astype(BF16)), dts, ef_ref, CHUNK - 1)
    wb_exp, totb_exp = _state_weights(_cumsum_rows(adts, upper.astype(BF16)), dts, eb_ref, 0)
    sf_ref[...] = jnp.zeros(sf_ref.shape, F32)
    sb_ref[...] = jnp.zeros(sb_ref.shape, F32)
    for c in range(n_chunks):
        _chunk_state_update(sf_ref, act_ref, c, wf_exp, totf_exp)
    for c in reversed(range(n_chunks)):
        _chunk_state_update(sb_ref, act_ref, c, wb_exp, totb_exp)


def _conv_scratch(rows):
    return [pltpu.VMEM((rows + 2 * BF16_ROWS, CONV_CH), BF16),
            pltpu.VMEM((2, CONV_K * CONV_WIN, CONV_CH), BF16)]


def _ctx_states(xbc_c, dt_c, smat, conv_w8, conv_b, alog, ef, eb, batch, ctx_len):
    n_chunks = ctx_len // CHUNK
    state = jax.ShapeDtypeStruct((batch, N_SSM_GROUPS, D_STATE, GROUP_W), F32)
    st_spec = pl.BlockSpec((None, N_SSM_GROUPS, D_STATE, GROUP_W), lambda b: (b, 0, 0, 0))
    return pl.pallas_call(
        functools.partial(_ctx_state_kernel, n_chunks),
        grid=(batch,),
        in_specs=[pl.BlockSpec((ctx_len, CONV_CH), lambda b: (b, 0)),
                  pl.BlockSpec((ctx_len, DT_W), lambda b: (b, 0)),
                  _const_spec(smat.shape), _const_spec((SUBLANES, CONV_CH)), _const_spec((1, CONV_CH)),
                  _const_spec((1, LANES)), _const_spec(ef.shape), _const_spec(eb.shape)],
        out_specs=[st_spec, st_spec],
        out_shape=[state, state],
        scratch_shapes=_conv_scratch(ctx_len) + [pltpu.VMEM((ctx_len, CONV_CH), BF16)],
        compiler_params=_cparams(("parallel",)),
        name="ssd_context_states",
    )(xbc_c, dt_c, smat, conv_w8, conv_b, alog, ef, eb)


def _bwd_sweep_kernel(n_blk, cur_ref, prev_ref, next_ref, dt_ref, smat_ref, cw_ref, cb_ref, alog_ref, eb_ref,
                      sb0_ref, act_ref, sin_ref, buf_ref, p_ref, state_ref):
    j = pl.program_id(1)
    blk = n_blk - 1 - j
    rows = SSD_ROWS
    n_chunks = rows // CHUNK

    @pl.when(j == 0)
    def _():
        state_ref[...] = sb0_ref[...]

    zeros = jnp.zeros((BF16_ROWS, CONV_CH), BF16)
    buf_ref[0:BF16_ROWS, :] = jnp.where(blk > 0, prev_ref[...], zeros)
    buf_ref[pl.ds(BF16_ROWS, rows), :] = cur_ref[...]
    buf_ref[pl.ds(BF16_ROWS + rows, BF16_ROWS), :] = jnp.where(blk < n_blk - 1, next_ref[...], zeros)
    _conv_silu(buf_ref, n_chunks, smat_ref, cw_ref, cb_ref, p_ref, act_ref)

    a_row = _neg_a_log2(alog_ref)
    _, upper = _tri()
    dts = [dt_ref[pl.ds(c * CHUNK, CHUNK), :] for c in range(n_chunks)]
    acs_b = _cumsum_rows([d * a_row for d in dts], upper.astype(BF16))
    w_exp, tot_exp = _state_weights(acs_b, dts, eb_ref, 0)
    for c in reversed(range(n_chunks)):
        sin_ref[c] = state_ref[...].astype(BF16)
        _chunk_state_update(state_ref, act_ref, c, w_exp, tot_exp)


def _bwd_sweep(xbc, dt, smat, conv_w8, conv_b, alog, eb, sb0, batch, seq_len):
    t = xbc.shape[0]
    rows = SSD_ROWS
    n_blk = seq_len // rows
    cpb = rows // CHUNK
    halo_per_blk = rows // BF16_ROWS
    n_halo = t // BF16_ROWS

    def blk_of(b, j):
        return b * n_blk + (n_blk - 1 - j)

    return pl.pallas_call(
        functools.partial(_bwd_sweep_kernel, n_blk),
        grid=(batch, n_blk),
        in_specs=[pl.BlockSpec((rows, CONV_CH), lambda b, j: (blk_of(b, j), 0)),
                  pl.BlockSpec((BF16_ROWS, CONV_CH),
                               lambda b, j: (jnp.maximum(blk_of(b, j) * halo_per_blk - 1, 0), 0)),
                  pl.BlockSpec((BF16_ROWS, CONV_CH),
                               lambda b, j: (jnp.minimum((blk_of(b, j) + 1) * halo_per_blk, n_halo - 1), 0)),
                  pl.BlockSpec((rows, DT_W), lambda b, j: (blk_of(b, j), 0)),
                  _const_spec(smat.shape), _const_spec((SUBLANES, CONV_CH)), _const_spec((1, CONV_CH)),
                  _const_spec((1, LANES)), _const_spec(eb.shape),
                  pl.BlockSpec((None, N_SSM_GROUPS, D_STATE, GROUP_W), lambda b, j: (b, 0, 0, 0))],
        out_specs=[pl.BlockSpec((rows, CONV_CH), lambda b, j: (blk_of(b, j), 0)),
                   pl.BlockSpec((cpb, N_SSM_GROUPS, D_STATE, GROUP_W), lambda b, j: (blk_of(b, j), 0, 0, 0))],
        out_shape=[jax.ShapeDtypeStruct((t, CONV_CH), BF16),
                   jax.ShapeDtypeStruct((t // CHUNK, N_SSM_GROUPS, D_STATE, GROUP_W), BF16)],
        scratch_shapes=_conv_scratch(rows) + [pltpu.VMEM((N_SSM_GROUPS, D_STATE, GROUP_W), F32)],
        compiler_params=_cparams(("parallel", "arbitrary")),
        name="ssd_backward_sweep",
    )(xbc, xbc, xbc, dt, smat, conv_w8, conv_b, alog, eb, sb0)


def _fwd_sweep_kernel(act_ref, dt_ref, z_ref, sbin_ref, sf0_ref, alog_ref, ef2_ref, eb2_ref, dskip_ref, ng_ref,
                      y_ref, state_ref, ybuf_ref):
    j = pl.program_id(1)

    @pl.when(j == 0)
    def _():
        state_ref[...] = sf0_ref[...]

    a_row = _neg_a_log2(alog_ref)
    lower, upper = _tri()
    lower_bf16 = lower.astype(BF16)
    upper_bf16 = upper.astype(BF16)
    diag = lower & upper
    lane = lax.broadcasted_iota(jnp.int32, (1, LANES), 1)
    lane_lo = lane < SSM_HEAD_DIM
    heads_per_group = N_SSM_HEADS // N_SSM_GROUPS

    n_chunks = SSD_ROWS // CHUNK
    dts = [dt_ref[pl.ds(c * CHUNK, CHUNK), :] for c in range(n_chunks)]
    adts = [d * a_row for d in dts]
    acs_f_all = _cumsum_rows(adts, lower_bf16)
    acs_b_all = _cumsum_rows(adts, upper_bf16)
    adt_ts = [a.T for a in adts]
    acst_f_all = _cumsum_cols(adt_ts, upper_bf16)
    acst_b_all = _cumsum_cols(adt_ts, lower_bf16)
    ef_all = _expand(jnp.concatenate([jnp.exp2(a) for a in acs_f_all], axis=0), ef2_ref)
    eb_all = _expand(jnp.concatenate([jnp.exp2(a) for a in acs_b_all], axis=0), eb2_ref)
    w_all = _expand(jnp.concatenate(
        [jnp.exp2(a[CHUNK - 1:CHUNK, :] - a) * d for a, d in zip(acs_f_all, dts)], axis=0), ef2_ref)

    for c in range(n_chunks):
        rows = pl.ds(c * CHUNK, CHUNK)
        crows = slice(c * CHUNK, (c + 1) * CHUNK)
        acs_f, acs_b = acs_f_all[c], acs_b_all[c]
        dt_t = dts[c].T
        log2_dt_t = jnp.log2(dt_t)
        row_f = acst_f_all[c] - log2_dt_t
        row_b = acst_b_all[c] - log2_dt_t
        diag_log = jnp.log2(dt_t[0:N_SSM_HEADS] + dt_t[N_SSM_HEADS:2 * N_SSM_HEADS])

        b_chunk = act_ref[rows, D_INNER:D_INNER + BC_W]
        c_chunk = act_ref[rows, D_INNER + BC_W:CONV_CH]

        for g in range(N_SSM_GROUPS):
            cb = _dot_nt(c_chunk[:, g * D_STATE:(g + 1) * D_STATE], b_chunk[:, g * D_STATE:(g + 1) * D_STATE])
            for pair in range(g * heads_per_group // 2, (g + 1) * heads_per_group // 2):
                xs_pair = act_ref[rows, pair * LANES:(pair + 1) * LANES]
                outs = []
                for h in (2 * pair, 2 * pair + 1):
                    hb = N_SSM_HEADS + h
                    seg_f = acs_f[:, h:h + 1] - row_f[h:h + 1, :]
                    seg_b = acs_b[:, hb:hb + 1] - row_b[hb:hb + 1, :]
                    seg = jnp.where(diag, diag_log[h:h + 1, :], jnp.where(lower, seg_f, seg_b))
                    outs.append(_dot((cb * jnp.exp2(seg)).astype(BF16), xs_pair))
                ybuf_ref[:, pair * LANES:(pair + 1) * LANES] = jnp.where(lane_lo, outs[0], outs[1])

        ef_exp = ef_all[crows]
        eb_exp = eb_all[crows]
        for g in range(N_SSM_GROUPS):
            cols = slice(g * GROUP_W, (g + 1) * GROUP_W)
            c_g = c_chunk[:, g * D_STATE:(g + 1) * D_STATE]
            y = (ybuf_ref[:, cols]
                 + _dot(c_g, state_ref[g].astype(BF16)) * ef_exp[:, cols]
                 + _dot(c_g, sbin_ref[c, g]) * eb_exp[:, cols])
            y = y + dskip_ref[:, cols] * act_ref[rows, cols].astype(F32)
            y = y * z_ref[rows, cols].astype(F32)
            ms = jnp.mean(y * y, axis=-1, keepdims=True)
            y_ref[rows, cols] = (y * lax.rsqrt(ms + EPS) * ng_ref[:, cols]).astype(BF16)

        _state_update(state_ref, b_chunk, act_ref[rows, 0:D_INNER], w_all[crows], ef_exp[CHUNK - 1:CHUNK, :])


def _fwd_sweep(act, dt, z, sb_in, sf0, alog, ef2, eb2, dskip, norm_g, batch, seq_len):
    t = act.shape[0]
    rows = SSD_ROWS
    n_blk = seq_len // rows
    cpb = rows // CHUNK
    blk = lambda b, j: b * n_blk + j
    return pl.pallas_call(
        _fwd_sweep_kernel,
        grid=(batch, n_blk),
        in_specs=[pl.BlockSpec((rows, CONV_CH), lambda b, j: (blk(b, j), 0)),
                  pl.BlockSpec((rows, DT_W), lambda b, j: (blk(b, j), 0)),
                  pl.BlockSpec((rows, D_INNER), lambda b, j: (blk(b, j), 0)),
                  pl.BlockSpec((cpb, N_SSM_GROUPS, D_STATE, GROUP_W), lambda b, j: (blk(b, j), 0, 0, 0)),
                  pl.BlockSpec((None, N_SSM_GROUPS, D_STATE, GROUP_W), lambda b, j: (b, 0, 0, 0)),
                  _const_spec((1, LANES)), _const_spec(ef2.shape), _const_spec(eb2.shape),
                  _const_spec((1, D_INNER)), _const_spec((1, D_INNER))],
        out_specs=pl.BlockSpec((rows, D_INNER), lambda b, j: (blk(b, j), 0)),
        out_shape=jax.ShapeDtypeStruct((t, D_INNER), BF16),
        scratch_shapes=[pltpu.VMEM((N_SSM_GROUPS, D_STATE, GROUP_W), F32),
                        pltpu.VMEM((CHUNK, D_INNER), F32)],
        compiler_params=_cparams(("parallel", "arbitrary")),
        name="ssd_forward_sweep",
    )(act, dt, z, sb_in, sf0, alog, ef2, eb2, dskip, norm_g)


def _attn_kernel(n_steps, sink_ref, q_ref, kv0_ref, kv1_ref, kv2_ref, kv3_ref, kvx_ref, o_ref):
    i = pl.program_id(1)
    group = N_Q_HEADS // N_KV_HEADS
    width = group * ATT_BLOCK
    key_j = lax.broadcasted_iota(jnp.int32, (ATT_BLOCK, width), 0)
    lane = lax.broadcasted_iota(jnp.int32, (ATT_BLOCK, width), 1)
    query_r = lane % ATT_BLOCK
    neg = jnp.float32(-1e30)
    band_prev = key_j >= query_r
    band_next = key_j <= query_r
    lane1 = lax.broadcasted_iota(jnp.int32, (1, width), 1)
    half = HEAD_DIM

    def swap_halves(t):
        return jnp.concatenate([t[half:], t[:half]], axis=0)

    for sub, kvh in [(sub, kvh) for sub in range(2) for kvh in range(N_KV_HEADS)]:
        qrows = slice(sub * ATT_BLOCK, (sub + 1) * ATT_BLOCK)
        kvp_ref, kvc_ref, kvn_ref = (kv0_ref, kv1_ref, kv2_ref) if sub == 0 else (kv1_ref, kv2_ref, kv3_ref)
        bias_prev = jnp.where(band_prev & (i > 0), 0.0, neg) if sub == 0 else jnp.where(band_prev, 0.0, neg)
        bias_next = jnp.where(band_next, 0.0, neg) if sub == 0 else jnp.where(band_next & (i < n_steps - 1), 0.0, neg)
        t0 = q_ref[qrows, (2 * kvh) * LANES:(2 * kvh + 1) * LANES].astype(F32).T.astype(BF16)
        t1 = q_ref[qrows, (2 * kvh + 1) * LANES:(2 * kvh + 2) * LANES].astype(F32).T.astype(BF16)
        q_t = jnp.concatenate([t0, swap_halves(t0), t1, swap_halves(t1)], axis=1)
        kc = slice(kvh * LANES, (kvh + 1) * LANES)
        vc = slice(V_OFF + kvh * LANES, V_OFF + (kvh + 1) * LANES)
        k_all = jnp.concatenate([kvp_ref[:, kc], kvc_ref[:, kc], kvn_ref[:, kc], kvx_ref[:, kc]], axis=0)
        v_all = jnp.concatenate([kvp_ref[:, vc], kvc_ref[:, vc], kvn_ref[:, vc], kvx_ref[:, vc]], axis=0)
        s = _dot(k_all, q_t)
        s_p = s[0:ATT_BLOCK] + bias_prev
        s_c = s[ATT_BLOCK:2 * ATT_BLOCK]
        s_n = s[2 * ATT_BLOCK:3 * ATT_BLOCK] + bias_next
        s_x = s[3 * ATT_BLOCK:]
        sink = jnp.zeros((1, width), F32)
        for g in range(group):
            sink = jnp.where(lane1 // ATT_BLOCK == g, sink_ref[kvh * group + g] * LOG2E, sink)
        red = lambda f, parts: functools.reduce(f, parts)
        m = red(jnp.maximum, [jnp.max(p, axis=0, keepdims=True) for p in (s_p, s_c, s_n, s_x)])
        m = jnp.maximum(m, sink)
        e = [jnp.exp2(p - m) for p in (s_p, s_c, s_n, s_x)]
        denom = red(jnp.add, [jnp.sum(p, axis=0, keepdims=True) for p in e]) + jnp.exp2(sink - m)
        p_all = jnp.concatenate([p.astype(BF16) for p in e], axis=0)
        v_t = v_all.T[:half]
        out_t = _dot(v_t, p_all) * (1.0 / denom)
        for sl in range(2):
            pair_t = jnp.concatenate([out_t[:, (2 * sl) * ATT_BLOCK:(2 * sl + 1) * ATT_BLOCK],
                                      out_t[:, (2 * sl + 1) * ATT_BLOCK:(2 * sl + 2) * ATT_BLOCK]], axis=0)
            o_ref[qrows, (2 * kvh + sl) * LANES:(2 * kvh + sl + 1) * LANES] = pair_t.T.astype(BF16)


def _attention(q, kv, kv_ctx, sink, batch, seq_len, ctx_len):
    t = q.shape[0]
    n_blk = seq_len // ATT_BLOCK
    n_steps = n_blk // 2
    qrows = 2 * ATT_BLOCK

    def kv_spec(offset):
        return pl.BlockSpec((ATT_BLOCK, KV_CAT),
                            lambda b, i: (b * n_blk + jnp.clip(2 * i + offset, 0, n_blk - 1), 0))

    return pl.pallas_call(
        functools.partial(_attn_kernel, n_steps),
        grid=(batch, n_steps),
        in_specs=[pl.BlockSpec(memory_space=pltpu.SMEM),
                  pl.BlockSpec((qrows, ATT_W), lambda b, i: (b * n_steps + i, 0)),
                  kv_spec(-1), kv_spec(0), kv_spec(1), kv_spec(2),
                  pl.BlockSpec((ctx_len, KV_CAT), lambda b, i: (b, 0))],
        out_specs=pl.BlockSpec((qrows, ATT_W), lambda b, i: (b * n_steps + i, 0)),
        out_shape=jax.ShapeDtypeStruct((t, ATT_W), BF16),
        compiler_params=_cparams(("parallel", "parallel")),
        name="window_attention",
    )(sink, q, kv, kv, kv, kv, kv_ctx)


FF_SPLIT = 1280


def _merge_ffn_kernel(x_ref, ya_ref, ys_ref, ga_ref, gb_ref, g1_ref, sc_ref, sh_ref, g2_ref, ng_ref, fg_ref,
                      woa_ref, wob_ref, wout_ref, win_ref, wffo_ref, o_ref):
    a = _dot(ya_ref[...], woa_ref[...])
    b = _dot(ys_ref[...], wob_ref[...])
    merged = (ga_ref[...].astype(F32) * a + gb_ref[...].astype(F32) * b).astype(BF16)
    x1 = x_ref[...] + g1_ref[...] * _dot(merged, wout_ref[...])

    h = _adaln(x1, ng_ref, sc_ref, sh_ref).astype(BF16)
    acc = None
    for c0, c1 in ((0, FF_SPLIT), (FF_SPLIT, D_FF)):
        gate = _dot(h, win_ref[:, c0:c1])
        up = _dot(h, win_ref[:, D_FF + c0:D_FF + c1])
        part = _dot((_silu(gate) * up).astype(BF16), wffo_ref[c0:c1, :])
        acc = part if acc is None else acc + part
    x2 = x1 + g2_ref[...] * acc
    ms = jnp.mean(x2 * x2, axis=-1, keepdims=True)
    o_ref[...] = x2 * lax.rsqrt(ms + EPS) * fg_ref[...]


def _merge_ffn(x2, y_att, y_ssm, ga, gb, g1, sc2, sh2, g2, norm_g, final_g, w_oa, w_ob, w_out, w_ffn_in, w_ffn_out,
               seq_len):
    t, d = x2.shape
    tm = ROW_TILE
    per_seq = seq_len // tm
    row = lambda w: pl.BlockSpec((tm, w), lambda i: (i, 0))
    mod_spec = pl.BlockSpec((None, 1, d), lambda i: (i // per_seq, 0, 0))
    return pl.pallas_call(
        _merge_ffn_kernel,
        grid=(t // tm,),
        in_specs=[row(d), row(ATT_W), row(D_INNER), row(d), row(d), mod_spec, mod_spec, mod_spec, mod_spec,
                  _const_spec((1, d)), _const_spec((1, d)),
                  _const_spec(w_oa.shape), _const_spec(w_ob.shape), _const_spec(w_out.shape),
                  _const_spec(w_ffn_in.shape), _const_spec(w_ffn_out.shape)],
        out_specs=row(d),
        out_shape=jax.ShapeDtypeStruct((t, d), F32),
        compiler_params=_cparams(("parallel",)),
        name="merge_ffn_final_norm",
    )(x2, y_att, y_ssm, ga, gb, g1, sc2, sh2, g2, norm_g, final_g, w_oa, w_ob, w_out, w_ffn_in, w_ffn_out)


def _rope_tables(seq_len):
    rows = seq_len // GRID_W
    row = jnp.repeat(jnp.arange(rows), GRID_W).astype(F32)
    col = jnp.tile(jnp.arange(GRID_W), rows).astype(F32)
    inv = ROPE_BASE ** (-jnp.arange(ROPE_FREQS, dtype=F32) / ROPE_FREQS)
    ang_r = row[:, None] * inv
    ang_c = col[:, None] * inv
    ang = jnp.concatenate([ang_r, ang_r, ang_c, ang_c] * 2, axis=-1)
    sign = np.where((np.arange(LANES) % (2 * ROPE_FREQS)) < ROPE_FREQS, -1.0, 1.0).astype(np.float32)
    return jnp.cos(ang), jnp.sin(ang) * jnp.asarray(sign)[None, :]


def _expand_matrix(first_row):
    e = np.zeros((LANES, D_INNER), np.float32)
    for h in range(N_SSM_HEADS):
        e[first_row + h, h * SSM_HEAD_DIM:(h + 1) * SSM_HEAD_DIM] = 1.0
    return jnp.asarray(e, dtype=BF16)


def _conv_shift_matrix():
    s = np.zeros((CHUNK, CONV_K * CONV_WIN), np.float32)
    for k in range(CONV_K):
        off = k + BF16_ROWS - CONV_K // 2 if k <= CONV_K // 2 else k - CONV_K // 2
        for t in range(CHUNK):
            s[t, k * CONV_WIN + t + off] = 1.0
    return jnp.asarray(s, dtype=BF16)


def kernel(x, c, ctx, c_ctx, w_mod, b_mod, norm1_g, w_in, attn_sink, conv_w, conv_b, a_log_f, a_log_b,
           dt_bias_f, dt_bias_b, d_skip, ssm_norm_g, w_oa, w_ob, w_out, norm2_g, w_ffn_in, w_ffn_out, final_g):
    batch, seq_len, d = x.shape
    ctx_len = ctx.shape[1]
    assert w_mod.shape[0] == 1 and d == D_MODEL
    assert seq_len % SSD_ROWS == 0 and seq_len % ROW_TILE == 0 and seq_len % GRID_W == 0
    assert seq_len % (2 * ATT_BLOCK) == 0
    assert ctx_len % CHUNK == 0 and ctx_len <= ROW_TILE

    pad_rows = (-(batch + 1)) % SUBLANES
    c_all = jnp.concatenate([c, c_ctx[None, :], jnp.zeros((pad_rows, d), F32)], axis=0)
    mod = _modulation(c_all, w_mod[0], b_mod[0][None, :])
    mod_b = mod[:batch].reshape(batch, N_MOD, 1, d)
    sh1, sc1, g1, sh2, sc2, g2 = (mod_b[:, k] for k in range(N_MOD))
    mod_c = mod[batch:batch + 1].reshape(1, N_MOD, 1, d)
    sh1c, sc1c = mod_c[:, 0], mod_c[:, 1]

    w = w_in[0]
    o = np.cumsum([0, ATT_W, KV_W, KV_W, D_INNER, CONV_CH, N_SSM_HEADS, N_SSM_HEADS, D_MODEL, D_MODEL])
    seg = lambda k: w[:, o[k]:o[k + 1]]
    w_main = jnp.concatenate(
        [seg(0), seg(1), seg(2), seg(3), seg(4), seg(7), seg(8), seg(5), seg(6),
         jnp.zeros((d, DT_W - 2 * N_SSM_HEADS), F32)], axis=1).astype(BF16)
    pad_heads = jnp.zeros((DT_W - 2 * N_SSM_HEADS,), F32)
    dt_bias = jnp.concatenate([dt_bias_f[0], dt_bias_b[0], pad_heads])[None, :]
    alog = jnp.concatenate([a_log_f[0], a_log_b[0], pad_heads])[None, :]
    conv_w8 = jnp.concatenate([conv_w[0], jnp.zeros((SUBLANES - CONV_K, CONV_CH), F32)], axis=0)
    conv_b1 = conv_b[0][None, :]
    cos, sin = _rope_tables(seq_len)
    ef2 = _expand_matrix(0)
    eb2 = _expand_matrix(N_SSM_HEADS)
    g_norm1 = norm1_g[0][None, :]

    x2 = x.reshape(batch * seq_len, d)
    xc2 = ctx.reshape(batch * ctx_len, d)

    q, kv, z, xbc, dt, ga, gb = _inproj(True, x2, sc1, sh1, g_norm1, w_main, dt_bias, cos, sin, seq_len)
    kv_c, xbc_c, dt_c = _inproj(False, xc2, sc1c, sh1c, g_norm1, w_main, dt_bias, None, None, ctx_len)

    smat = _conv_shift_matrix()
    sf0, sb0 = _ctx_states(xbc_c, dt_c, smat, conv_w8, conv_b1, alog, ef2, eb2, batch, ctx_len)
    act, sb_in = _bwd_sweep(xbc, dt, smat, conv_w8, conv_b1, alog, eb2, sb0, batch, seq_len)
    dskip = jnp.repeat(d_skip[0], SSM_HEAD_DIM)[None, :]
    y_ssm = _fwd_sweep(act, dt, z, sb_in, sf0, alog, ef2, eb2, dskip, ssm_norm_g[0][None, :], batch, seq_len)

    y_att = _attention(q, kv, kv_c, attn_sink[0], batch, seq_len, ctx_len)

    out = _merge_ffn(x2, y_att, y_ssm, ga, gb, g1, sc2, sh2, g2, norm2_g[0][None, :], final_g[None, :],
                     w_oa[0].astype(BF16), w_ob[0].astype(BF16), w_out[0].astype(BF16),
                     w_ffn_in[0].astype(BF16), w_ffn_out[0].astype(BF16), seq_len)
    return out.reshape(batch, seq_len, d)
```

```python
import functools

import numpy as np
import jax
import jax.numpy as jnp
from jax import lax
from jax.experimental import pallas as pl
from jax.experimental.pallas import tpu as pltpu

F32 = jnp.float32
BF16 = jnp.bfloat16

D_MODEL = 1024
GRID_W = 64
EPS = 1e-6
HEAD_DIM = 64
N_Q_HEADS = 16
N_KV_HEADS = 4
ATT_W = N_Q_HEADS * HEAD_DIM
KV_W = N_KV_HEADS * HEAD_DIM
WINDOW = 128
ROPE_BASE = 10000.0
ROPE_FREQS = HEAD_DIM // 4
D_INNER = 2 * D_MODEL
SSM_HEAD_DIM = 64
N_SSM_HEADS = D_INNER // SSM_HEAD_DIM
N_SSM_GROUPS = 4
GROUP_W = D_INNER // N_SSM_GROUPS
D_STATE = 128
BC_W = N_SSM_GROUPS * D_STATE
CONV_K = 5
CONV_CH = D_INNER + 2 * BC_W
CHUNK = 128
CONV_WIN = CHUNK + 16
D_FF = -(-8 * D_MODEL // (3 * 256)) * 256
N_MOD = 6

LANES = 128
SUBLANES = 8
BF16_ROWS = 16
VMEM_LIMIT = 56 * 1024 * 1024

OFF_Q = 0
OFF_K = OFF_Q + ATT_W
OFF_V = OFF_K + KV_W
OFF_Z = OFF_V + KV_W
OFF_XBC = OFF_Z + D_INNER
OFF_GA = OFF_XBC + CONV_CH
OFF_GB = OFF_GA + D_MODEL
OFF_DT = OFF_GB + D_MODEL
DT_W = LANES
N_PROJ = OFF_DT + DT_W
KV_CAT = 2 * N_KV_HEADS * LANES
V_OFF = N_KV_HEADS * LANES
LOG2E = 1.4426950408889634

ROW_TILE = 512
SSD_ROWS = 512
ATT_BLOCK = 128


def _cparams(sem):
    return pltpu.CompilerParams(dimension_semantics=sem, vmem_limit_bytes=VMEM_LIMIT)


def _const_spec(shape):
    n = len(shape)
    return pl.BlockSpec(shape, lambda *_: (0,) * n, pipeline_mode=pl.Buffered(1))


def _dot(a, b):
    return jnp.dot(a, b, preferred_element_type=F32)


def _dot_nt(a, b):
    return lax.dot_general(a, b, (((1,), (1,)), ((), ())), preferred_element_type=F32)


def _dot_tn(a, b):
    return lax.dot_general(a, b, (((0,), (0,)), ((), ())), preferred_element_type=F32)


def _sigmoid(v):
    return 1.0 / (1.0 + jnp.exp(-v))


def _silu(v):
    return v * _sigmoid(v)


def _split3(v):
    hi = v.astype(BF16)
    r = v - hi.astype(F32)
    mid = r.astype(BF16)
    lo = (r - mid.astype(F32)).astype(BF16)
    return hi, mid, lo


def _mod_kernel(c_ref, w_ref, b_ref, o_ref):
    act = _silu(c_ref[...])
    o_ref[...] = jnp.dot(act, w_ref[...], preferred_element_type=F32,
                         precision=lax.Precision.HIGHEST) + b_ref[...]


def _modulation(c_all, w_mod, b_mod):
    rows, d = c_all.shape
    n = w_mod.shape[1]
    tn = 1024
    return pl.pallas_call(
        _mod_kernel,
        grid=(n // tn,),
        in_specs=[pl.BlockSpec((rows, d), lambda j: (0, 0)),
                  pl.BlockSpec((d, tn), lambda j: (0, j)),
                  pl.BlockSpec((1, tn), lambda j: (0, j))],
        out_specs=pl.BlockSpec((rows, tn), lambda j: (0, j)),
        out_shape=jax.ShapeDtypeStruct((rows, n), F32),
        compiler_params=_cparams(("arbitrary",)),
        name="modulation",
    )(c_all, w_mod, b_mod)


def _adaln(x, g_ref, sc_ref, sh_ref):
    ms = jnp.mean(x * x, axis=-1, keepdims=True)
    xn = x * lax.rsqrt(ms + EPS)
    return xn * (g_ref[...] * (1.0 + sc_ref[...])) + sh_ref[...]


def _store_padded_heads(val, col0, kv_ref):
    lane = lax.broadcasted_iota(jnp.int32, (1, LANES), 1)
    is_lo = lane < HEAD_DIM
    for s in range(2):
        slab = val[:, s * LANES:(s + 1) * LANES]
        swapped = pltpu.roll(slab, HEAD_DIM, 1)
        for h, src in ((2 * s, slab), (2 * s + 1, swapped)):
            kv_ref[:, col0 + h * LANES: col0 + (h + 1) * LANES] = jnp.where(is_lo, src, 0.0).astype(BF16)


def _rope_slab(slab, cos, sin_signed, first_half):
    fwd = pltpu.roll(slab, LANES - ROPE_FREQS, 1)
    bwd = pltpu.roll(slab, ROPE_FREQS, 1)
    return slab * cos + jnp.where(first_half, fwd, bwd) * sin_signed


PROJ_PIECE = 1024


def _softplus(v):
    return jnp.maximum(v, 0.0) + jnp.log1p(jnp.exp(-jnp.abs(v)))


def _project_kv(proj, rope, kv_ref):
    k = proj(OFF_K, KV_W)
    if rope is not None:
        k = jnp.concatenate([_rope_slab(k[:, s * LANES:(s + 1) * LANES], *rope) for s in range(2)], axis=1)
    _store_padded_heads(k, 0, kv_ref)
    _store_padded_heads(proj(OFF_V, KV_W), V_OFF, kv_ref)


def _inproj_ctx_kernel(x_ref, sc_ref, sh_ref, g_ref, w_ref, dtb_ref, kv_ref, xbc_ref, dt_ref):
    h = _adaln(x_ref[...], g_ref, sc_ref, sh_ref).astype(BF16)
    proj = lambda off, width: _dot(h, w_ref[:, off:off + width])
    _project_kv(proj, None, kv_ref)
    for c0 in range(0, CONV_CH, PROJ_PIECE):
        xbc_ref[:, c0:c0 + PROJ_PIECE] = proj(OFF_XBC + c0, PROJ_PIECE).astype(BF16)
    dt_ref[...] = _softplus(proj(OFF_DT, DT_W) + dtb_ref[...])


def _inproj_context(xc2, sc, sh, g, w_main, dt_bias, ctx_len):
    t, d = xc2.shape
    tm = ctx_len
    mod_spec = pl.BlockSpec((None, 1, d), lambda i: (0, 0, 0))
    row = lambda w: pl.BlockSpec((tm, w), lambda i: (i, 0))
    sds = lambda w, dt: jax.ShapeDtypeStruct((t, w), dt)
    return pl.pallas_call(
        _inproj_ctx_kernel,
        grid=(t // tm,),
        in_specs=[row(d), mod_spec, mod_spec, _const_spec((1, d)), _const_spec((d, N_PROJ)),
                  _const_spec((1, DT_W))],
        out_specs=[row(KV_CAT), row(CONV_CH), row(DT_W)],
        out_shape=[sds(KV_CAT, BF16), sds(CONV_CH, BF16), sds(DT_W, F32)],
        compiler_params=_cparams(("parallel",)),
        name="inproj_context",
    )(xc2, sc, sh, g, w_main, dt_bias)


def _attn_scores(q_slabs, k_parts):
    half = HEAD_DIM

    def swap_halves(t):
        return jnp.concatenate([t[half:], t[:half]], axis=0)

    t0, t1 = (q.astype(F32).T.astype(BF16) for q in q_slabs)
    q_t = jnp.concatenate([t0, swap_halves(t0), t1, swap_halves(t1)], axis=1)
    return _dot(jnp.concatenate(k_parts, axis=0), q_t)


def _attn_finish(s, v_parts, bias_prev, bias_next, sink_row):
    half = HEAD_DIM
    s_p = s[0:ATT_BLOCK] + bias_prev
    s_c = s[ATT_BLOCK:2 * ATT_BLOCK]
    s_n = s[2 * ATT_BLOCK:3 * ATT_BLOCK] + bias_next
    s_x = s[3 * ATT_BLOCK:]
    red = lambda f, parts: functools.reduce(f, parts)
    m = red(jnp.maximum, [jnp.max(p, axis=0, keepdims=True) for p in (s_p, s_c, s_n, s_x)])
    m = jnp.maximum(m, sink_row)
    e = [jnp.exp2(p - m) for p in (s_p, s_c, s_n, s_x)]
    denom = red(jnp.add, [jnp.sum(p, axis=0, keepdims=True) for p in e]) + jnp.exp2(sink_row - m)
    p_all = jnp.concatenate([p.astype(BF16) for p in e], axis=0)
    v_t = jnp.concatenate(v_parts, axis=0).T[:half]
    out_t = _dot(v_t, p_all) * (1.0 / denom)
    return [jnp.concatenate([out_t[:, (2 * sl) * ATT_BLOCK:(2 * sl + 1) * ATT_BLOCK],
                             out_t[:, (2 * sl + 1) * ATT_BLOCK:(2 * sl + 2) * ATT_BLOCK]], axis=0).T
            for sl in range(2)]


def _inproj_attn_kernel(per_seq, sink_ref, x_ref, sc_ref, sh_ref, g_ref, w_ref, dtb_ref, cos_ref, sin_ref, kvx_ref,
                        z_ref, xbc_ref, dt_ref, ga_ref, gb_ref, ya_ref,
                        q_cur, q_nxt, kv_cur, kv_nxt, kv_tail):
    i = pl.program_id(0)
    blocks = ROW_TILE // ATT_BLOCK
    group = N_Q_HEADS // N_KV_HEADS
    width = group * ATT_BLOCK

    @pl.when(i == 0)
    def _():
        q_cur[...] = jnp.zeros(q_cur.shape, BF16)
        kv_cur[...] = jnp.zeros(kv_cur.shape, BF16)
        kv_tail[...] = jnp.zeros(kv_tail.shape, BF16)

    h = _adaln(x_ref[...], g_ref, sc_ref, sh_ref).astype(BF16)
    proj = lambda off, wid: _dot(h, w_ref[:, off:off + wid])
    cos = cos_ref[...]
    sin = sin_ref[...]
    lane = lax.broadcasted_iota(jnp.int32, (1, LANES), 1)
    first_half = (lane % (2 * ROPE_FREQS)) < ROPE_FREQS
    _project_kv(proj, (cos, sin, first_half), kv_nxt)
    scale = HEAD_DIM ** -0.5 * LOG2E
    q = proj(OFF_Q, ATT_W)
    for s in range(ATT_W // LANES):
        q_nxt[:, s * LANES:(s + 1) * LANES] = (
            _rope_slab(q[:, s * LANES:(s + 1) * LANES], cos, sin, first_half) * scale).astype(BF16)

    piece = PROJ_PIECE // 2

    def store_xbc(c0):
        xbc_ref[:, c0:c0 + piece] = proj(OFF_XBC + c0, piece).astype(BF16)

    def store_z(c0):
        z_ref[:, c0:c0 + piece] = _silu(proj(OFF_Z + c0, piece)).astype(BF16)

    def store_dt():
        dt_ref[...] = _softplus(proj(OFF_DT, DT_W) + dtb_ref[...])

    def store_gate(ref, off, c0):
        ref[:, c0:c0 + piece] = _sigmoid(proj(off + c0, piece)).astype(BF16)

    segments = ([functools.partial(store_xbc, c0) for c0 in range(0, CONV_CH, piece)] + [store_dt]
                + [functools.partial(store_z, c0) for c0 in range(0, D_INNER, piece)]
                + [functools.partial(store_gate, ga_ref, OFF_GA, c0) for c0 in range(0, D_MODEL, piece)]
                + [functools.partial(store_gate, gb_ref, OFF_GB, c0) for c0 in range(0, D_MODEL, piece)])

    seq_tile = (i + per_seq - 1) % per_seq
    key_j = lax.broadcasted_iota(jnp.int32, (ATT_BLOCK, width), 0)
    lane_w = lax.broadcasted_iota(jnp.int32, (ATT_BLOCK, width), 1)
    query_r = lane_w % ATT_BLOCK
    neg = jnp.float32(-1e30)
    band_prev = key_j >= query_r
    band_next = key_j <= query_r
    lane1 = lax.broadcasted_iota(jnp.int32, (1, width), 1)

    def parts(sub, kvh, col0):
        qrows = slice(sub * ATT_BLOCK, (sub + 1) * ATT_BLOCK)
        cols = slice(col0 + kvh * LANES, col0 + (kvh + 1) * LANES)
        prev = kv_tail[:, cols] if sub == 0 else kv_cur[(sub - 1) * ATT_BLOCK:sub * ATT_BLOCK, cols]
        nxt = (kv_nxt[0:ATT_BLOCK, cols] if sub == blocks - 1
               else kv_cur[(sub + 1) * ATT_BLOCK:(sub + 2) * ATT_BLOCK, cols])
        return [prev, kv_cur[qrows, cols], nxt, kvx_ref[:, cols]]

    def scores(sub, kvh):
        qrows = slice(sub * ATT_BLOCK, (sub + 1) * ATT_BLOCK)
        q_slabs = [q_cur[qrows, (2 * kvh + sl) * LANES:(2 * kvh + sl + 1) * LANES] for sl in range(2)]
        return _attn_scores(q_slabs, parts(sub, kvh, 0))

    work = [(sub, kvh) for sub in range(blocks) for kvh in range(N_KV_HEADS)]
    s_next = scores(*work[0])
    for n, (sub, kvh) in enumerate(work):
        s_cur = s_next
        if n + 1 < len(work):
            s_next = scores(*work[n + 1])
        if segments:
            segments.pop(0)()
        qrows = slice(sub * ATT_BLOCK, (sub + 1) * ATT_BLOCK)
        prev_ok = (seq_tile > 0) if sub == 0 else True
        next_ok = (seq_tile < per_seq - 1) if sub == blocks - 1 else True
        bias_prev = jnp.where(band_prev & prev_ok, 0.0, neg)
        bias_next = jnp.where(band_next & next_ok, 0.0, neg)
        sink_row = jnp.zeros((1, width), F32)
        for g in range(group):
            sink_row = jnp.where(lane1 // ATT_BLOCK == g, sink_ref[kvh * group + g] * LOG2E, sink_row)
        outs = _attn_finish(s_cur, parts(sub, kvh, V_OFF), bias_prev, bias_next, sink_row)
        for sl in range(2):
            ya_ref[qrows, (2 * kvh + sl) * LANES:(2 * kvh + sl + 1) * LANES] = outs[sl].astype(BF16)
    for seg in segments:
        seg()

    kv_tail[...] = kv_cur[ROW_TILE - ATT_BLOCK:, :]
    kv_cur[...] = kv_nxt[...]
    q_cur[...] = q_nxt[...]


def _inproj_attention(x2, sc, sh, g, w_main, dt_bias, cos, sin, kv_ctx, sink, seq_len, ctx_len):
    t, d = x2.shape
    tm = ROW_TILE
    per_seq = seq_len // tm
    n_tiles = t // tm
    cur = lambda i: jnp.minimum(i, n_tiles - 1)
    prev = lambda i: jnp.maximum(i - 1, 0)
    mod_spec = pl.BlockSpec((None, 1, d), lambda i: (cur(i) // per_seq, 0, 0))
    pos_spec = pl.BlockSpec((tm, LANES), lambda i: (cur(i) % per_seq, 0))
    row = lambda w: pl.BlockSpec((tm, w), lambda i: (cur(i), 0))
    sds = lambda w, dt: jax.ShapeDtypeStruct((t, w), dt)
    return pl.pallas_call(
        functools.partial(_inproj_attn_kernel, per_seq),
        grid=(n_tiles + 1,),
        in_specs=[pl.BlockSpec(memory_space=pltpu.SMEM), row(d), mod_spec, mod_spec, _const_spec((1, d)),
                  _const_spec((d, N_PROJ)), _const_spec((1, DT_W)), pos_spec, pos_spec,
                  pl.BlockSpec((ctx_len, KV_CAT), lambda i: (prev(i) // per_seq, 0))],
        out_specs=[row(D_INNER), row(CONV_CH), row(DT_W), row(d), row(d),
                   pl.BlockSpec((tm, ATT_W), lambda i: (prev(i), 0))],
        out_shape=[sds(D_INNER, BF16), sds(CONV_CH, BF16), sds(DT_W, F32), sds(d, BF16), sds(d, BF16),
                   sds(ATT_W, BF16)],
        scratch_shapes=[pltpu.VMEM((tm, ATT_W), BF16), pltpu.VMEM((tm, ATT_W), BF16),
                        pltpu.VMEM((tm, KV_CAT), BF16), pltpu.VMEM((tm, KV_CAT), BF16),
                        pltpu.VMEM((ATT_BLOCK, KV_CAT), BF16)],
        compiler_params=_cparams(("arbitrary",)),
        name="inproj_attention",
    )(sink, x2, sc, sh, g, w_main, dt_bias, cos, sin, kv_ctx)


def _conv_silu(buf_ref, n_blocks, smat_ref, w_ref, b_ref, p_ref, out_ref):
    for m in range(n_blocks):
        r0 = m * CHUNK
        slot = m % 2
        for k in range(CONV_K):
            start = r0 if k <= CONV_K // 2 else r0 + BF16_ROWS
            p_ref[slot, k * CONV_WIN:(k + 1) * CONV_WIN, :] = (
                buf_ref[pl.ds(start, CONV_WIN), :] * w_ref[k:k + 1, :].astype(BF16))
        acc = _dot(smat_ref[...], p_ref[slot]) + b_ref[...]
        out_ref[pl.ds(r0, CHUNK), :] = _silu(acc).astype(out_ref.dtype)


def _tri():
    r = lax.broadcasted_iota(jnp.int32, (CHUNK, CHUNK), 0)
    c = lax.broadcasted_iota(jnp.int32, (CHUNK, CHUNK), 1)
    return r >= c, r <= c


def _cumsum_rows(adts, tri_bf16):
    x3 = jnp.concatenate([t for a in adts for t in _split3(a)], axis=1)
    c = _dot(tri_bf16, x3)
    w = 3 * LANES
    return [c[:, i * w:i * w + LANES] + c[:, i * w + LANES:i * w + 2 * LANES] + c[:, i * w + 2 * LANES:(i + 1) * w]
            for i in range(len(adts))]


def _cumsum_cols(adt_ts, tri_bf16):
    x3 = jnp.concatenate([t for a in adt_ts for t in _split3(a)], axis=0)
    c = _dot(x3, tri_bf16)
    h = 3 * CHUNK
    return [c[i * h:i * h + CHUNK] + c[i * h + CHUNK:i * h + 2 * CHUNK] + c[i * h + 2 * CHUNK:(i + 1) * h]
            for i in range(len(adt_ts))]


def _expand(v, e_ref):
    return _dot(v.astype(BF16), e_ref[...])


def _state_update(state_ref, b_chunk, xs_chunk, w_exp, tot_exp):
    for g in range(N_SSM_GROUPS):
        cols = slice(g * GROUP_W, (g + 1) * GROUP_W)
        xw = (xs_chunk[:, cols].astype(F32) * w_exp[:, cols]).astype(BF16)
        upd = _dot_tn(b_chunk[:, g * D_STATE:(g + 1) * D_STATE], xw)
        state_ref[g] = state_ref[g] * tot_exp[:, cols] + upd


def _neg_a_log2(alog_ref):
    return -jnp.exp(alog_ref[...]) * LOG2E


def _state_weights(acs, dts, e_ref, total_row):
    totals = [a[total_row:total_row + 1, :] for a in acs]
    w = jnp.concatenate([jnp.exp2(t - a) * d for t, a, d in zip(totals, acs, dts)], axis=0)
    pad = jnp.zeros((SUBLANES - len(acs), LANES), F32)
    tot = jnp.concatenate([jnp.exp2(t) for t in totals] + [pad], axis=0)
    return _expand(w, e_ref), _expand(tot, e_ref)


def _chunk_state_update(state_ref, act_ref, c, w_exp, tot_exp):
    rows = pl.ds(c * CHUNK, CHUNK)
    _state_update(state_ref, act_ref[rows, D_INNER:D_INNER + BC_W], act_ref[rows, 0:D_INNER],
                  w_exp[c * CHUNK:(c + 1) * CHUNK], tot_exp[c:c + 1])


def _ctx_state_kernel(n_chunks, xbc_ref, dt_ref, smat_ref, cw_ref, cb_ref, alog_ref, ef_ref, eb_ref,
                      sf_ref, sb_ref, buf_ref, p_ref, act_ref):
    rows = n_chunks * CHUNK
    zeros = jnp.zeros((BF16_ROWS, CONV_CH), BF16)
    buf_ref[0:BF16_ROWS, :] = zeros
    buf_ref[pl.ds(BF16_ROWS, rows), :] = xbc_ref[...]
    buf_ref[pl.ds(BF16_ROWS + rows, BF16_ROWS), :] = zeros
    _conv_silu(buf_ref, n_chunks, smat_ref, cw_ref, cb_ref, p_ref, act_ref)

    a_row = _neg_a_log2(alog_ref)
    lower, upper = _tri()
    dts = [dt_ref[pl.ds(c * CHUNK, CHUNK), :] for c in range(n_chunks)]
    adts = [d * a_row for d in dts]
    wf_exp, totf_exp = _state_weights(_cumsum_rows(adts, lower.astype(BF16)), dts, ef_ref, CHUNK - 1)
    wb_exp, totb_exp = _state_weights(_cumsum_rows(adts, upper.astype(BF16)), dts, eb_ref, 0)
    sf_ref[...] = jnp.zeros(sf_ref.shape, F32)
    sb_ref[...] = jnp.zeros(sb_ref.shape, F32)
    for c in range(n_chunks):
        _chunk_state_update(sf_ref, act_ref, c, wf_exp, totf_exp)
    for c in reversed(range(n_chunks)):
        _chunk_state_update(sb_ref, act_ref, c, wb_exp, totb_exp)


def _conv_scratch(rows):
    return [pltpu.VMEM((rows + 2 * BF16_ROWS, CONV_CH), BF16),
            pltpu.VMEM((2, CONV_K * CONV_WIN, CONV_CH), BF16)]


def _ctx_states(xbc_c, dt_c, smat, conv_w8, conv_b, alog, ef, eb, batch, ctx_len):
    n_chunks = ctx_len // CHUNK
    state = jax.ShapeDtypeStruct((batch, N_SSM_GROUPS, D_STATE, GROUP_W), F32)
    st_spec = pl.BlockSpec((None, N_SSM_GROUPS, D_STATE, GROUP_W), lambda b: (b, 0, 0, 0))
    return pl.pallas_call(
        functools.partial(_ctx_state_kernel, n_chunks),
        grid=(batch,),
        in_specs=[pl.BlockSpec((ctx_len, CONV_CH), lambda b: (b, 0)),
                  pl.BlockSpec((ctx_len, DT_W), lambda b: (b, 0)),
                  _const_spec(smat.shape), _const_spec((SUBLANES, CONV_CH)), _const_spec((1, CONV_CH)),
                  _const_spec((1, LANES)), _const_spec(ef.shape), _const_spec(eb.shape)],
        out_specs=[st_spec, st_spec],
        out_shape=[state, state],
        scratch_shapes=_conv_scratch(ctx_len) + [pltpu.VMEM((ctx_len, CONV_CH), BF16)],
        compiler_params=_cparams(("parallel",)),
        name="ssd_context_states",
    )(xbc_c, dt_c, smat, conv_w8, conv_b, alog, ef, eb)


def _bwd_sweep_kernel(n_blk, cur_ref, prev_ref, next_ref, dt_ref, smat_ref, cw_ref, cb_ref, alog_ref, eb_ref,
                      sb0_ref, act_ref, sin_ref, buf_ref, p_ref, state_ref):
    j = pl.program_id(1)
    blk = n_blk - 1 - j
    rows = SSD_ROWS
    n_chunks = rows // CHUNK

    @pl.when(j == 0)
    def _():
        state_ref[...] = sb0_ref[...]

    zeros = jnp.zeros((BF16_ROWS, CONV_CH), BF16)
    buf_ref[0:BF16_ROWS, :] = jnp.where(blk > 0, prev_ref[...], zeros)
    buf_ref[pl.ds(BF16_ROWS, rows), :] = cur_ref[...]
    buf_ref[pl.ds(BF16_ROWS + rows, BF16_ROWS), :] = jnp.where(blk < n_blk - 1, next_ref[...], zeros)
    _conv_silu(buf_ref, n_chunks, smat_ref, cw_ref, cb_ref, p_ref, act_ref)

    a_row = _neg_a_log2(alog_ref)
    _, upper = _tri()
    dts = [dt_ref[pl.ds(c * CHUNK, CHUNK), :] for c in range(n_chunks)]
    acs_b = _cumsum_rows([d * a_row for d in dts], upper.astype(BF16))
    w_exp, tot_exp = _state_weights(acs_b, dts, eb_ref, 0)
    for c in reversed(range(n_chunks)):
        sin_ref[c] = state_ref[...].astype(BF16)
        _chunk_state_update(state_ref, act_ref, c, w_exp, tot_exp)


def _bwd_sweep(xbc, dt, smat, conv_w8, conv_b, alog, eb, sb0, batch, seq_len):
    t = xbc.shape[0]
    rows = SSD_ROWS
    n_blk = seq_len // rows
    cpb = rows // CHUNK
    halo_per_blk = rows // BF16_ROWS
    n_halo = t // BF16_ROWS

    def blk_of(b, j):
        return b * n_blk + (n_blk - 1 - j)

    return pl.pallas_call(
        functools.partial(_bwd_sweep_kernel, n_blk),
        grid=(batch, n_blk),
        in_specs=[pl.BlockSpec((rows, CONV_CH), lambda b, j: (blk_of(b, j), 0)),
                  pl.BlockSpec((BF16_ROWS, CONV_CH),
                               lambda b, j: (jnp.maximum(blk_of(b, j) * halo_per_blk - 1, 0), 0)),
                  pl.BlockSpec((BF16_ROWS, CONV_CH),
                               lambda b, j: (jnp.minimum((blk_of(b, j) + 1) * halo_per_blk, n_halo - 1), 0)),
                  pl.BlockSpec((rows, DT_W), lambda b, j: (blk_of(b, j), 0)),
                  _const_spec(smat.shape), _const_spec((SUBLANES, CONV_CH)), _const_spec((1, CONV_CH)),
                  _const_spec((1, LANES)), _const_spec(eb.shape),
                  pl.BlockSpec((None, N_SSM_GROUPS, D_STATE, GROUP_W), lambda b, j: (b, 0, 0, 0))],
        out_specs=[pl.BlockSpec((rows, CONV_CH), lambda b, j: (blk_of(b, j), 0)),
                   pl.BlockSpec((cpb, N_SSM_GROUPS, D_STATE, GROUP_W), lambda b, j: (blk_of(b, j), 0, 0, 0))],
        out_shape=[jax.ShapeDtypeStruct((t, CONV_CH), BF16),
                   jax.ShapeDtypeStruct((t // CHUNK, N_SSM_GROUPS, D_STATE, GROUP_W), BF16)],
        scratch_shapes=_conv_scratch(rows) + [pltpu.VMEM((N_SSM_GROUPS, D_STATE, GROUP_W), F32)],
        compiler_params=_cparams(("parallel", "arbitrary")),
        name="ssd_backward_sweep",
    )(xbc, xbc, xbc, dt, smat, conv_w8, conv_b, alog, eb, sb0)


def _fwd_sweep_kernel(act_ref, dt_ref, z_ref, sbin_ref, sf0_ref, alog_ref, ef2_ref, eb2_ref, dskip_ref, ng_ref,
                      y_ref, state_ref, ybuf_ref):
    j = pl.program_id(1)

    @pl.when(j == 0)
    def _():
        state_ref[...] = sf0_ref[...]

    a_row = _neg_a_log2(alog_ref)
    lower, upper = _tri()
    lower_bf16 = lower.astype(BF16)
    upper_bf16 = upper.astype(BF16)
    diag = lower & upper
    lane = lax.broadcasted_iota(jnp.int32, (1, LANES), 1)
    lane_lo = lane < SSM_HEAD_DIM
    heads_per_group = N_SSM_HEADS // N_SSM_GROUPS

    n_chunks = SSD_ROWS // CHUNK
    dts = [dt_ref[pl.ds(c * CHUNK, CHUNK), :] for c in range(n_chunks)]
    adts = [d * a_row for d in dts]
    acs_f_all = _cumsum_rows(adts, lower_bf16)
    acs_b_all = _cumsum_rows(adts, upper_bf16)
    adt_ts = [a.T for a in adts]
    acst_f_all = _cumsum_cols(adt_ts, upper_bf16)
    acst_b_all = _cumsum_cols(adt_ts, lower_bf16)
    ef_all = _expand(jnp.concatenate([jnp.exp2(a) for a in acs_f_all], axis=0), ef2_ref)
    eb_all = _expand(jnp.concatenate([jnp.exp2(a) for a in acs_b_all], axis=0), eb2_ref)
    w_all = _expand(jnp.concatenate(
        [jnp.exp2(a[CHUNK - 1:CHUNK, :] - a) * d for a, d in zip(acs_f_all, dts)], axis=0), ef2_ref)

    for c in range(n_chunks):
        rows = pl.ds(c * CHUNK, CHUNK)
        crows = slice(c * CHUNK, (c + 1) * CHUNK)
        acs_f, acs_b = acs_f_all[c], acs_b_all[c]
        dt_t = dts[c].T
        log2_dt_t = jnp.log2(dt_t)
        row_f = acst_f_all[c] - log2_dt_t
        row_b = acst_b_all[c] - log2_dt_t
        diag_log = jnp.log2(dt_t[0:N_SSM_HEADS] + dt_t[N_SSM_HEADS:2 * N_SSM_HEADS])

        b_chunk = act_ref[rows, D_INNER:D_INNER + BC_W]
        c_chunk = act_ref[rows, D_INNER + BC_W:CONV_CH]

        for g in range(N_SSM_GROUPS):
            cb = _dot_nt(c_chunk[:, g * D_STATE:(g + 1) * D_STATE], b_chunk[:, g * D_STATE:(g + 1) * D_STATE])
            for pair in range(g * heads_per_group // 2, (g + 1) * heads_per_group // 2):
                xs_pair = act_ref[rows, pair * LANES:(pair + 1) * LANES]
                outs = []
                for h in (2 * pair, 2 * pair + 1):
                    hb = N_SSM_HEADS + h
                    seg_f = acs_f[:, h:h + 1] - row_f[h:h + 1, :]
                    seg_b = acs_b[:, hb:hb + 1] - row_b[hb:hb + 1, :]
                    seg = jnp.where(diag, diag_log[h:h + 1, :], jnp.where(lower, seg_f, seg_b))
                    outs.append(_dot((cb * jnp.exp2(seg)).astype(BF16), xs_pair))
                ybuf_ref[:, pair * LANES:(pair + 1) * LANES] = jnp.where(lane_lo, outs[0], outs[1])

        ef_exp = ef_all[crows]
        eb_exp = eb_all[crows]
        for g in range(N_SSM_GROUPS):
            cols = slice(g * GROUP_W, (g + 1) * GROUP_W)
            c_g = c_chunk[:, g * D_STATE:(g + 1) * D_STATE]
            y = (ybuf_ref[:, cols]
                 + _dot(c_g, state_ref[g].astype(BF16)) * ef_exp[:, cols]
                 + _dot(c_g, sbin_ref[c, g]) * eb_exp[:, cols])
            y = y + dskip_ref[:, cols] * act_ref[rows, cols].astype(F32)
            y = y * z_ref[rows, cols].astype(F32)
            ms = jnp.mean(y * y, axis=-1, keepdims=True)
            y_ref[rows, cols] = (y * lax.rsqrt(ms + EPS) * ng_ref[:, cols]).astype(BF16)

        _state_update(state_ref, b_chunk, act_ref[rows, 0:D_INNER], w_all[crows], ef_exp[CHUNK - 1:CHUNK, :])


def _fwd_sweep(act, dt, z, sb_in, sf0, alog, ef2, eb2, dskip, norm_g, batch, seq_len):
    t = act.shape[0]
    rows = SSD_ROWS
    n_blk = seq_len // rows
    cpb = rows // CHUNK
    blk = lambda b, j: b * n_blk + j
    return pl.pallas_call(
        _fwd_sweep_kernel,
        grid=(batch, n_blk),
        in_specs=[pl.BlockSpec((rows, CONV_CH), lambda b, j: (blk(b, j), 0)),
                  pl.BlockSpec((rows, DT_W), lambda b, j: (blk(b, j), 0)),
                  pl.BlockSpec((rows, D_INNER), lambda b, j: (blk(b, j), 0)),
                  pl.BlockSpec((cpb, N_SSM_GROUPS, D_STATE, GROUP_W), lambda b, j: (blk(b, j), 0, 0, 0)),
                  pl.BlockSpec((None, N_SSM_GROUPS, D_STATE, GROUP_W), lambda b, j: (b, 0, 0, 0)),
                  _const_spec((1, LANES)), _const_spec(ef2.shape), _const_spec(eb2.shape),
                  _const_spec((1, D_INNER)), _const_spec((1, D_INNER))],
        out_specs=pl.BlockSpec((rows, D_INNER), lambda b, j: (blk(b, j), 0)),
        out_shape=jax.ShapeDtypeStruct((t, D_INNER), BF16),
        scratch_shapes=[pltpu.VMEM((N_SSM_GROUPS, D_STATE, GROUP_W), F32),
                        pltpu.VMEM((CHUNK, D_INNER), F32)],
        compiler_params=_cparams(("parallel", "arbitrary")),
        name="ssd_forward_sweep",
    )(act, dt, z, sb_in, sf0, alog, ef2, eb2, dskip, norm_g)


FF_SPLIT = 1280


def _merge_ffn_kernel(x_ref, ya_ref, ys_ref, ga_ref, gb_ref, g1_ref, sc_ref, sh_ref, g2_ref, ng_ref, fg_ref,
                      woa_ref, wob_ref, wout_ref, win_ref, wffo_ref, o_ref):
    a = _dot(ya_ref[...], woa_ref[...])
    b = _dot(ys_ref[...], wob_ref[...])
    merged = (ga_ref[...].astype(F32) * a + gb_ref[...].astype(F32) * b).astype(BF16)
    x1 = x_ref[...] + g1_ref[...] * _dot(merged, wout_ref[...])

    h = _adaln(x1, ng_ref, sc_ref, sh_ref).astype(BF16)
    acc = None
    for c0, c1 in ((0, FF_SPLIT), (FF_SPLIT, D_FF)):
        gate = _dot(h, win_ref[:, c0:c1])
        up = _dot(h, win_ref[:, D_FF + c0:D_FF + c1])
        part = _dot((_silu(gate) * up).astype(BF16), wffo_ref[c0:c1, :])
        acc = part if acc is None else acc + part
    x2 = x1 + g2_ref[...] * acc
    ms = jnp.mean(x2 * x2, axis=-1, keepdims=True)
    o_ref[...] = x2 * lax.rsqrt(ms + EPS) * fg_ref[...]


def _merge_ffn(x2, y_att, y_ssm, ga, gb, g1, sc2, sh2, g2, norm_g, final_g, w_oa, w_ob, w_out, w_ffn_in, w_ffn_out,
               seq_len):
    t, d = x2.shape
    tm = ROW_TILE
    per_seq = seq_len // tm
    row = lambda w: pl.BlockSpec((tm, w), lambda i: (i, 0))
    mod_spec = pl.BlockSpec((None, 1, d), lambda i: (i // per_seq, 0, 0))
    return pl.pallas_call(
        _merge_ffn_kernel,
        grid=(t // tm,),
        in_specs=[row(d), row(ATT_W), row(D_INNER), row(d), row(d), mod_spec, mod_spec, mod_spec, mod_spec,
                  _const_spec((1, d)), _const_spec((1, d)),
                  _const_spec(w_oa.shape), _const_spec(w_ob.shape), _const_spec(w_out.shape),
                  _const_spec(w_ffn_in.shape), _const_spec(w_ffn_out.shape)],
        out_specs=row(d),
        out_shape=jax.ShapeDtypeStruct((t, d), F32),
        compiler_params=_cparams(("parallel",)),
        name="merge_ffn_final_norm",
    )(x2, y_att, y_ssm, ga, gb, g1, sc2, sh2, g2, norm_g, final_g, w_oa, w_ob, w_out, w_ffn_in, w_ffn_out)


def _rope_tables(seq_len):
    rows = seq_len // GRID_W
    row = jnp.repeat(jnp.arange(rows), GRID_W).astype(F32)
    col = jnp.tile(jnp.arange(GRID_W), rows).astype(F32)
    inv = ROPE_BASE ** (-jnp.arange(ROPE_FREQS, dtype=F32) / ROPE_FREQS)
    ang_r = row[:, None] * inv
    ang_c = col[:, None] * inv
    ang = jnp.concatenate([ang_r, ang_r, ang_c, ang_c] * 2, axis=-1)
    sign = np.where((np.arange(LANES) % (2 * ROPE_FREQS)) < ROPE_FREQS, -1.0, 1.0).astype(np.float32)
    return jnp.cos(ang), jnp.sin(ang) * jnp.asarray(sign)[None, :]


def _expand_matrix(first_row):
    e = np.zeros((LANES, D_INNER), np.float32)
    for h in range(N_SSM_HEADS):
        e[first_row + h, h * SSM_HEAD_DIM:(h + 1) * SSM_HEAD_DIM] = 1.0
    return jnp.asarray(e, dtype=BF16)


def _conv_shift_matrix():
    s = np.zeros((CHUNK, CONV_K * CONV_WIN), np.float32)
    for k in range(CONV_K):
        off = k + BF16_ROWS - CONV_K // 2 if k <= CONV_K // 2 else k - CONV_K // 2
        for t in range(CHUNK):
            s[t, k * CONV_WIN + t + off] = 1.0
    return jnp.asarray(s, dtype=BF16)


def kernel(x, c, ctx, c_ctx, w_mod, b_mod, norm1_g, w_in, attn_sink, conv_w, conv_b, a_log_f, a_log_b,
           dt_bias_f, dt_bias_b, d_skip, ssm_norm_g, w_oa, w_ob, w_out, norm2_g, w_ffn_in, w_ffn_out, final_g):
    batch, seq_len, d = x.shape
    ctx_len = ctx.shape[1]
    assert w_mod.shape[0] == 1 and d == D_MODEL
    assert seq_len % SSD_ROWS == 0 and seq_len % ROW_TILE == 0 and seq_len % GRID_W == 0
    assert ctx_len % CHUNK == 0

    pad_rows = (-(batch + 1)) % SUBLANES
    c_all = jnp.concatenate([c, c_ctx[None, :], jnp.zeros((pad_rows, d), F32)], axis=0)
    mod = _modulation(c_all, w_mod[0], b_mod[0][None, :])
    mod_b = mod[:batch].reshape(batch, N_MOD, 1, d)
    sh1, sc1, g1, sh2, sc2, g2 = (mod_b[:, k] for k in range(N_MOD))
    mod_c = mod[batch:batch + 1].reshape(1, N_MOD, 1, d)
    sh1c, sc1c = mod_c[:, 0], mod_c[:, 1]

    w = w_in[0]
    o = np.cumsum([0, ATT_W, KV_W, KV_W, D_INNER, CONV_CH, N_SSM_HEADS, N_SSM_HEADS, D_MODEL, D_MODEL])
    seg = lambda k: w[:, o[k]:o[k + 1]]
    w_main = jnp.concatenate(
        [seg(0), seg(1), seg(2), seg(3), seg(4), seg(7), seg(8), seg(5), seg(6),
         jnp.zeros((d, DT_W - 2 * N_SSM_HEADS), F32)], axis=1).astype(BF16)
    pad_heads = jnp.zeros((DT_W - 2 * N_SSM_HEADS,), F32)
    dt_bias = jnp.concatenate([dt_bias_f[0], dt_bias_b[0], pad_heads])[None, :]
    alog = jnp.concatenate([a_log_f[0], a_log_b[0], pad_heads])[None, :]
    conv_w8 = jnp.concatenate([conv_w[0], jnp.zeros((SUBLANES - CONV_K, CONV_CH), F32)], axis=0)
    conv_b1 = conv_b[0][None, :]
    cos, sin = _rope_tables(seq_len)
    ef2 = _expand_matrix(0)
    eb2 = _expand_matrix(N_SSM_HEADS)
    g_norm1 = norm1_g[0][None, :]

    x2 = x.reshape(batch * seq_len, d)
    xc2 = ctx.reshape(batch * ctx_len, d)

    kv_c, xbc_c, dt_c = _inproj_context(xc2, sc1c, sh1c, g_norm1, w_main, dt_bias, ctx_len)
    z, xbc, dt, ga, gb, y_att = _inproj_attention(x2, sc1, sh1, g_norm1, w_main, dt_bias, cos, sin, kv_c,
                                                  attn_sink[0], seq_len, ctx_len)

    smat = _conv_shift_matrix()
    sf0, sb0 = _ctx_states(xbc_c, dt_c, smat, conv_w8, conv_b1, alog, ef2, eb2, batch, ctx_len)
    act, sb_in = _bwd_sweep(xbc, dt, smat, conv_w8, conv_b1, alog, eb2, sb0, batch, seq_len)
    dskip = jnp.repeat(d_skip[0], SSM_HEAD_DIM)[None, :]
    y_ssm = _fwd_sweep(act, dt, z, sb_in, sf0, alog, ef2, eb2, dskip, ssm_norm_g[0][None, :], batch, seq_len)

    out = _merge_ffn(x2, y_att, y_ssm, ga, gb, g1, sc2, sh2, g2, norm2_g[0][None, :], final_g[None, :],
                     w_oa[0].astype(BF16), w_ob[0].astype(BF16), w_out[0].astype(BF16),
                     w_ffn_in[0].astype(BF16), w_ffn_out[0].astype(BF16), seq_len)
    return out.reshape(batch, seq_len, d)
```

```python
import functools

import numpy as np
import jax
import jax.numpy as jnp
from jax import lax
from jax.experimental import pallas as pl
from jax.experimental.pallas import tpu as pltpu

F32 = jnp.float32
BF16 = jnp.bfloat16

D_MODEL = 1024
GRID_W = 64
EPS = 1e-6
HEAD_DIM = 64
N_Q_HEADS = 16
N_KV_HEADS = 4
ATT_W = N_Q_HEADS * HEAD_DIM
KV_W = N_KV_HEADS * HEAD_DIM
WINDOW = 128
ROPE_BASE = 10000.0
ROPE_FREQS = HEAD_DIM // 4
D_INNER = 2 * D_MODEL
SSM_HEAD_DIM = 64
N_SSM_HEADS = D_INNER // SSM_HEAD_DIM
N_SSM_GROUPS = 4
GROUP_W = D_INNER // N_SSM_GROUPS
D_STATE = 128
BC_W = N_SSM_GROUPS * D_STATE
CONV_K = 5
CONV_CH = D_INNER + 2 * BC_W
CHUNK = 128
CONV_WIN = CHUNK + 16
D_FF = -(-8 * D_MODEL // (3 * 256)) * 256
N_MOD = 6

LANES = 128
SUBLANES = 8
BF16_ROWS = 16
VMEM_LIMIT = 56 * 1024 * 1024

OFF_Q = 0
OFF_K = OFF_Q + ATT_W
OFF_V = OFF_K + KV_W
OFF_Z = OFF_V + KV_W
OFF_XBC = OFF_Z + D_INNER
OFF_GA = OFF_XBC + CONV_CH
OFF_GB = OFF_GA + D_MODEL
OFF_DT = OFF_GB + D_MODEL
DT_W = LANES
N_PROJ = OFF_DT + DT_W
KV_CAT = 2 * N_KV_HEADS * LANES
V_OFF = N_KV_HEADS * LANES
LOG2E = 1.4426950408889634

ROW_TILE = 512
SSD_ROWS = 512
ATT_BLOCK = 128


def _cparams(sem):
    return pltpu.CompilerParams(dimension_semantics=sem, vmem_limit_bytes=VMEM_LIMIT)


def _const_spec(shape):
    n = len(shape)
    return pl.BlockSpec(shape, lambda *_: (0,) * n, pipeline_mode=pl.Buffered(1))


def _dot(a, b):
    return jnp.dot(a, b, preferred_element_type=F32)


def _dot_nt(a, b):
    return lax.dot_general(a, b, (((1,), (1,)), ((), ())), preferred_element_type=F32)


def _dot_tn(a, b):
    return lax.dot_general(a, b, (((0,), (0,)), ((), ())), preferred_element_type=F32)


def _sigmoid(v):
    return 1.0 / (1.0 + jnp.exp(-v))


def _silu(v):
    return v * _sigmoid(v)


def _split3(v):
    hi = v.astype(BF16)
    r = v - hi.astype(F32)
    mid = r.astype(BF16)
    lo = (r - mid.astype(F32)).astype(BF16)
    return hi, mid, lo


def _mod_kernel(c_ref, w_ref, b_ref, o_ref):
    act = _silu(c_ref[...])
    o_ref[...] = jnp.dot(act, w_ref[...], preferred_element_type=F32,
                         precision=lax.Precision.HIGHEST) + b_ref[...]


def _modulation(c_all, w_mod, b_mod):
    rows, d = c_all.shape
    n = w_mod.shape[1]
    tn = 1024
    return pl.pallas_call(
        _mod_kernel,
        grid=(n // tn,),
        in_specs=[pl.BlockSpec((rows, d), lambda j: (0, 0)),
                  pl.BlockSpec((d, tn), lambda j: (0, j)),
                  pl.BlockSpec((1, tn), lambda j: (0, j))],
        out_specs=pl.BlockSpec((rows, tn), lambda j: (0, j)),
        out_shape=jax.ShapeDtypeStruct((rows, n), F32),
        compiler_params=_cparams(("arbitrary",)),
        name="modulation",
    )(c_all, w_mod, b_mod)


def _adaln(x, g_ref, sc_ref, sh_ref):
    ms = jnp.mean(x * x, axis=-1, keepdims=True)
    xn = x * lax.rsqrt(ms + EPS)
    return xn * (g_ref[...] * (1.0 + sc_ref[...])) + sh_ref[...]


def _store_padded_heads(val, col0, kv_ref):
    lane = lax.broadcasted_iota(jnp.int32, (1, LANES), 1)
    is_lo = lane < HEAD_DIM
    for s in range(2):
        slab = val[:, s * LANES:(s + 1) * LANES]
        swapped = pltpu.roll(slab, HEAD_DIM, 1)
        for h, src in ((2 * s, slab), (2 * s + 1, swapped)):
            kv_ref[:, col0 + h * LANES: col0 + (h + 1) * LANES] = jnp.where(is_lo, src, 0.0).astype(BF16)


def _rope_slab(slab, cos, sin_signed, first_half):
    fwd = pltpu.roll(slab, LANES - ROPE_FREQS, 1)
    bwd = pltpu.roll(slab, ROPE_FREQS, 1)
    return slab * cos + jnp.where(first_half, fwd, bwd) * sin_signed


PROJ_PIECE = 1024


def _softplus(v):
    return jnp.maximum(v, 0.0) + jnp.log1p(jnp.exp(-jnp.abs(v)))


def _project_kv(proj, rope, kv_ref):
    k = proj(OFF_K, KV_W)
    if rope is not None:
        k = jnp.concatenate([_rope_slab(k[:, s * LANES:(s + 1) * LANES], *rope) for s in range(2)], axis=1)
    _store_padded_heads(k, 0, kv_ref)
    _store_padded_heads(proj(OFF_V, KV_W), V_OFF, kv_ref)


def _inproj_ctx_kernel(x_ref, sc_ref, sh_ref, g_ref, w_ref, dtb_ref, kv_ref, xbc_ref, dt_ref):
    h = _adaln(x_ref[...], g_ref, sc_ref, sh_ref).astype(BF16)
    proj = lambda off, width: _dot(h, w_ref[:, off:off + width])
    _project_kv(proj, None, kv_ref)
    for c0 in range(0, CONV_CH, PROJ_PIECE):
        xbc_ref[:, c0:c0 + PROJ_PIECE] = proj(OFF_XBC + c0, PROJ_PIECE).astype(BF16)
    dt_ref[...] = _softplus(proj(OFF_DT, DT_W) + dtb_ref[...])


def _inproj_context(xc2, sc, sh, g, w_main, dt_bias, ctx_len):
    t, d = xc2.shape
    tm = ctx_len
    mod_spec = pl.BlockSpec((None, 1, d), lambda i: (0, 0, 0))
    row = lambda w: pl.BlockSpec((tm, w), lambda i: (i, 0))
    sds = lambda w, dt: jax.ShapeDtypeStruct((t, w), dt)
    return pl.pallas_call(
        _inproj_ctx_kernel,
        grid=(t // tm,),
        in_specs=[row(d), mod_spec, mod_spec, _const_spec((1, d)), _const_spec((d, N_PROJ)),
                  _const_spec((1, DT_W))],
        out_specs=[row(KV_CAT), row(CONV_CH), row(DT_W)],
        out_shape=[sds(KV_CAT, BF16), sds(CONV_CH, BF16), sds(DT_W, F32)],
        compiler_params=_cparams(("parallel",)),
        name="inproj_context",
    )(xc2, sc, sh, g, w_main, dt_bias)


def _attn_scores(q_slabs, k_parts):
    half = HEAD_DIM

    def swap_halves(t):
        return jnp.concatenate([t[half:], t[:half]], axis=0)

    t0, t1 = (q.astype(F32).T.astype(BF16) for q in q_slabs)
    q_t = jnp.concatenate([t0, swap_halves(t0), t1, swap_halves(t1)], axis=1)
    return _dot(jnp.concatenate(k_parts, axis=0), q_t)


def _attn_finish(s, v_parts, bias_prev, bias_next, sink_row):
    half = HEAD_DIM
    s_p = s[0:ATT_BLOCK] + bias_prev
    s_c = s[ATT_BLOCK:2 * ATT_BLOCK]
    s_n = s[2 * ATT_BLOCK:3 * ATT_BLOCK] + bias_next
    s_x = s[3 * ATT_BLOCK:]
    red = lambda f, parts: functools.reduce(f, parts)
    m = red(jnp.maximum, [jnp.max(p, axis=0, keepdims=True) for p in (s_p, s_c, s_n, s_x)])
    m = jnp.maximum(m, sink_row)
    e = [jnp.exp2(p - m) for p in (s_p, s_c, s_n, s_x)]
    denom = red(jnp.add, [jnp.sum(p, axis=0, keepdims=True) for p in e]) + jnp.exp2(sink_row - m)
    p_all = jnp.concatenate([p.astype(BF16) for p in e], axis=0)
    v_t = jnp.concatenate(v_parts, axis=0).T[:half]
    out_t = _dot(v_t, p_all) * (1.0 / denom)
    return [jnp.concatenate([out_t[:, (2 * sl) * ATT_BLOCK:(2 * sl + 1) * ATT_BLOCK],
                             out_t[:, (2 * sl + 1) * ATT_BLOCK:(2 * sl + 2) * ATT_BLOCK]], axis=0).T
            for sl in range(2)]


def _inproj_attn_kernel(per_seq, sink_ref, x_ref, sc_ref, sh_ref, g_ref, w_ref, dtb_ref, cos_ref, sin_ref, kvx_ref,
                        z_ref, xbc_ref, dt_ref, ga_ref, gb_ref, ya_ref,
                        q_cur, q_nxt, kv_cur, kv_nxt, kv_tail):
    i = pl.program_id(0)
    blocks = ROW_TILE // ATT_BLOCK
    group = N_Q_HEADS // N_KV_HEADS
    width = group * ATT_BLOCK

    @pl.when(i == 0)
    def _():
        q_cur[...] = jnp.zeros(q_cur.shape, BF16)
        kv_cur[...] = jnp.zeros(kv_cur.shape, BF16)
        kv_tail[...] = jnp.zeros(kv_tail.shape, BF16)

    h = _adaln(x_ref[...], g_ref, sc_ref, sh_ref).astype(BF16)
    proj = lambda off, wid: _dot(h, w_ref[:, off:off + wid])
    cos = cos_ref[...]
    sin = sin_ref[...]
    lane = lax.broadcasted_iota(jnp.int32, (1, LANES), 1)
    first_half = (lane % (2 * ROPE_FREQS)) < ROPE_FREQS
    _project_kv(proj, (cos, sin, first_half), kv_nxt)
    scale = HEAD_DIM ** -0.5 * LOG2E
    q = proj(OFF_Q, ATT_W)
    for s in range(ATT_W // LANES):
        q_nxt[:, s * LANES:(s + 1) * LANES] = (
            _rope_slab(q[:, s * LANES:(s + 1) * LANES], cos, sin, first_half) * scale).astype(BF16)

    piece = PROJ_PIECE // 2

    def store_xbc(c0):
        xbc_ref[:, c0:c0 + piece] = proj(OFF_XBC + c0, piece).astype(BF16)

    def store_z(c0):
        z_ref[:, c0:c0 + piece] = _silu(proj(OFF_Z + c0, piece)).astype(BF16)

    def store_dt():
        dt_ref[...] = _softplus(proj(OFF_DT, DT_W) + dtb_ref[...])

    def store_gate(ref, off, c0):
        ref[:, c0:c0 + piece] = _sigmoid(proj(off + c0, piece)).astype(BF16)

    segments = ([functools.partial(store_xbc, c0) for c0 in range(0, CONV_CH, piece)] + [store_dt]
                + [functools.partial(store_z, c0) for c0 in range(0, D_INNER, piece)]
                + [functools.partial(store_gate, ga_ref, OFF_GA, c0) for c0 in range(0, D_MODEL, piece)]
                + [functools.partial(store_gate, gb_ref, OFF_GB, c0) for c0 in range(0, D_MODEL, piece)])

    seq_tile = (i + per_seq - 1) % per_seq
    key_j = lax.broadcasted_iota(jnp.int32, (ATT_BLOCK, width), 0)
    lane_w = lax.broadcasted_iota(jnp.int32, (ATT_BLOCK, width), 1)
    query_r = lane_w % ATT_BLOCK
    neg = jnp.float32(-1e30)
    band_prev = key_j >= query_r
    band_next = key_j <= query_r
    lane1 = lax.broadcasted_iota(jnp.int32, (1, width), 1)

    def parts(sub, kvh, col0):
        qrows = slice(sub * ATT_BLOCK, (sub + 1) * ATT_BLOCK)
        cols = slice(col0 + kvh * LANES, col0 + (kvh + 1) * LANES)
        prev = kv_tail[:, cols] if sub == 0 else kv_cur[(sub - 1) * ATT_BLOCK:sub * ATT_BLOCK, cols]
        nxt = (kv_nxt[0:ATT_BLOCK, cols] if sub == blocks - 1
               else kv_cur[(sub + 1) * ATT_BLOCK:(sub + 2) * ATT_BLOCK, cols])
        return [prev, kv_cur[qrows, cols], nxt, kvx_ref[:, cols]]

    def scores(sub, kvh):
        qrows = slice(sub * ATT_BLOCK, (sub + 1) * ATT_BLOCK)
        q_slabs = [q_cur[qrows, (2 * kvh + sl) * LANES:(2 * kvh + sl + 1) * LANES] for sl in range(2)]
        return _attn_scores(q_slabs, parts(sub, kvh, 0))

    work = [(sub, kvh) for sub in range(blocks) for kvh in range(N_KV_HEADS)]
    s_next = scores(*work[0])
    for n, (sub, kvh) in enumerate(work):
        s_cur = s_next
        if n + 1 < len(work):
            s_next = scores(*work[n + 1])
        if segments:
            segments.pop(0)()
        qrows = slice(sub * ATT_BLOCK, (sub + 1) * ATT_BLOCK)
        prev_ok = (seq_tile > 0) if sub == 0 else True
        next_ok = (seq_tile < per_seq - 1) if sub == blocks - 1 else True
        bias_prev = jnp.where(band_prev & prev_ok, 0.0, neg)
        bias_next = jnp.where(band_next & next_ok, 0.0, neg)
        sink_row = jnp.zeros((1, width), F32)
        for g in range(group):
            sink_row = jnp.where(lane1 // ATT_BLOCK == g, sink_ref[kvh * group + g] * LOG2E, sink_row)
        outs = _attn_finish(s_cur, parts(sub, kvh, V_OFF), bias_prev, bias_next, sink_row)
        for sl in range(2):
            ya_ref[qrows, (2 * kvh + sl) * LANES:(2 * kvh + sl + 1) * LANES] = outs[sl].astype(BF16)
    for seg in segments:
        seg()

    kv_tail[...] = kv_cur[ROW_TILE - ATT_BLOCK:, :]
    kv_cur[...] = kv_nxt[...]
    q_cur[...] = q_nxt[...]


def _inproj_attention(x2, sc, sh, g, w_main, dt_bias, cos, sin, kv_ctx, sink, seq_len, ctx_len):
    t, d = x2.shape
    tm = ROW_TILE
    per_seq = seq_len // tm
    n_tiles = t // tm
    cur = lambda i: jnp.minimum(i, n_tiles - 1)
    prev = lambda i: jnp.maximum(i - 1, 0)
    mod_spec = pl.BlockSpec((None, 1, d), lambda i: (cur(i) // per_seq, 0, 0))
    pos_spec = pl.BlockSpec((tm, LANES), lambda i: (cur(i) % per_seq, 0))
    row = lambda w: pl.BlockSpec((tm, w), lambda i: (cur(i), 0))
    sds = lambda w, dt: jax.ShapeDtypeStruct((t, w), dt)
    return pl.pallas_call(
        functools.partial(_inproj_attn_kernel, per_seq),
        grid=(n_tiles + 1,),
        in_specs=[pl.BlockSpec(memory_space=pltpu.SMEM), row(d), mod_spec, mod_spec, _const_spec((1, d)),
                  _const_spec((d, N_PROJ)), _const_spec((1, DT_W)), pos_spec, pos_spec,
                  pl.BlockSpec((ctx_len, KV_CAT), lambda i: (prev(i) // per_seq, 0))],
        out_specs=[row(D_INNER), row(CONV_CH), row(DT_W), row(d), row(d),
                   pl.BlockSpec((tm, ATT_W), lambda i: (prev(i), 0))],
        out_shape=[sds(D_INNER, BF16), sds(CONV_CH, BF16), sds(DT_W, F32), sds(d, BF16), sds(d, BF16),
                   sds(ATT_W, BF16)],
        scratch_shapes=[pltpu.VMEM((tm, ATT_W), BF16), pltpu.VMEM((tm, ATT_W), BF16),
                        pltpu.VMEM((tm, KV_CAT), BF16), pltpu.VMEM((tm, KV_CAT), BF16),
                        pltpu.VMEM((ATT_BLOCK, KV_CAT), BF16)],
        compiler_params=_cparams(("arbitrary",)),
        name="inproj_attention",
    )(sink, x2, sc, sh, g, w_main, dt_bias, cos, sin, kv_ctx)


def _conv_silu(buf_ref, n_blocks, smat_ref, w_ref, b_ref, p_ref, out_ref):
    for m in range(n_blocks):
        r0 = m * CHUNK
        slot = m % 2
        for k in range(CONV_K):
            start = r0 if k <= CONV_K // 2 else r0 + BF16_ROWS
            p_ref[slot, k * CONV_WIN:(k + 1) * CONV_WIN, :] = (
                buf_ref[pl.ds(start, CONV_WIN), :] * w_ref[k:k + 1, :].astype(BF16))
        acc = _dot(smat_ref[...], p_ref[slot]) + b_ref[...]
        out_ref[pl.ds(r0, CHUNK), :] = _silu(acc).astype(out_ref.dtype)


def _tri():
    r = lax.broadcasted_iota(jnp.int32, (CHUNK, CHUNK), 0)
    c = lax.broadcasted_iota(jnp.int32, (CHUNK, CHUNK), 1)
    return r >= c, r <= c


def _cumsum_rows(adts, tri_bf16):
    x3 = jnp.concatenate([t for a in adts for t in _split3(a)], axis=1)
    c = _dot(tri_bf16, x3)
    w = 3 * LANES
    return [c[:, i * w:i * w + LANES] + c[:, i * w + LANES:i * w + 2 * LANES] + c[:, i * w + 2 * LANES:(i + 1) * w]
            for i in range(len(adts))]


def _cumsum_cols(adt_ts, tri_bf16):
    x3 = jnp.concatenate([t for a in adt_ts for t in _split3(a)], axis=0)
    c = _dot(x3, tri_bf16)
    h = 3 * CHUNK
    return [c[i * h:i * h + CHUNK] + c[i * h + CHUNK:i * h + 2 * CHUNK] + c[i * h + 2 * CHUNK:(i + 1) * h]
            for i in range(len(adt_ts))]


def _expand(v, e_ref):
    return _dot(v.astype(BF16), e_ref[...])


def _state_update(state_ref, b_chunk, xs_chunk, w_exp, tot_exp):
    for g in range(N_SSM_GROUPS):
        cols = slice(g * GROUP_W, (g + 1) * GROUP_W)
        xw = (xs_chunk[:, cols].astype(F32) * w_exp[:, cols]).astype(BF16)
        upd = _dot_tn(b_chunk[:, g * D_STATE:(g + 1) * D_STATE], xw)
        state_ref[g] = state_ref[g] * tot_exp[:, cols] + upd


def _neg_a_log2(alog_ref):
    return -jnp.exp(alog_ref[...]) * LOG2E


def _state_weights(acs, dts, e_ref, total_row):
    totals = [a[total_row:total_row + 1, :] for a in acs]
    w = jnp.concatenate([jnp.exp2(t - a) * d for t, a, d in zip(totals, acs, dts)], axis=0)
    pad = jnp.zeros((SUBLANES - len(acs), LANES), F32)
    tot = jnp.concatenate([jnp.exp2(t) for t in totals] + [pad], axis=0)
    return _expand(w, e_ref), _expand(tot, e_ref)


def _chunk_state_update(state_ref, act_ref, c, w_exp, tot_exp):
    rows = pl.ds(c * CHUNK, CHUNK)
    _state_update(state_ref, act_ref[rows, D_INNER:D_INNER + BC_W], act_ref[rows, 0:D_INNER],
                  w_exp[c * CHUNK:(c + 1) * CHUNK], tot_exp[c:c + 1])


def _ctx_state_kernel(n_chunks, xbc_ref, dt_ref, smat_ref, cw_ref, cb_ref, alog_ref, ef_ref, eb_ref,
                      sf_ref, sb_ref, buf_ref, p_ref, act_ref):
    rows = n_chunks * CHUNK
    zeros = jnp.zeros((BF16_ROWS, CONV_CH), BF16)
    buf_ref[0:BF16_ROWS, :] = zeros
    buf_ref[pl.ds(BF16_ROWS, rows), :] = xbc_ref[...]
    buf_ref[pl.ds(BF16_ROWS + rows, BF16_ROWS), :] = zeros
    _conv_silu(buf_ref, n_chunks, smat_ref, cw_ref, cb_ref, p_ref, act_ref)

    a_row = _neg_a_log2(alog_ref)
    lower, upper = _tri()
    dts = [dt_ref[pl.ds(c * CHUNK, CHUNK), :] for c in range(n_chunks)]
    adts = [d * a_row for d in dts]
    wf_exp, totf_exp = _state_weights(_cumsum_rows(adts, lower.astype(BF16)), dts, ef_ref, CHUNK - 1)
    wb_exp, totb_exp = _state_weights(_cumsum_rows(adts, upper.astype(BF16)), dts, eb_ref, 0)
    sf_ref[...] = jnp.zeros(sf_ref.shape, F32)
    sb_ref[...] = jnp.zeros(sb_ref.shape, F32)
    for c in range(n_chunks):
        _chunk_state_update(sf_ref, act_ref, c, wf_exp, totf_exp)
    for c in reversed(range(n_chunks)):
        _chunk_state_update(sb_ref, act_ref, c, wb_exp, totb_exp)


def _conv_scratch(rows):
    return [pltpu.VMEM((rows + 2 * BF16_ROWS, CONV_CH), BF16),
            pltpu.VMEM((2, CONV_K * CONV_WIN, CONV_CH), BF16)]


def _ctx_states(xbc_c, dt_c, smat, conv_w8, conv_b, alog, ef, eb, batch, ctx_len):
    n_chunks = ctx_len // CHUNK
    state = jax.ShapeDtypeStruct((batch, N_SSM_GROUPS, D_STATE, GROUP_W), F32)
    st_spec = pl.BlockSpec((None, N_SSM_GROUPS, D_STATE, GROUP_W), lambda b: (b, 0, 0, 0))
    return pl.pallas_call(
        functools.partial(_ctx_state_kernel, n_chunks),
        grid=(batch,),
        in_specs=[pl.BlockSpec((ctx_len, CONV_CH), lambda b: (b, 0)),
                  pl.BlockSpec((ctx_len, DT_W), lambda b: (b, 0)),
                  _const_spec(smat.shape), _const_spec((SUBLANES, CONV_CH)), _const_spec((1, CONV_CH)),
                  _const_spec((1, LANES)), _const_spec(ef.shape), _const_spec(eb.shape)],
        out_specs=[st_spec, st_spec],
        out_shape=[state, state],
        scratch_shapes=_conv_scratch(ctx_len) + [pltpu.VMEM((ctx_len, CONV_CH), BF16)],
        compiler_params=_cparams(("parallel",)),
        name="ssd_context_states",
    )(xbc_c, dt_c, smat, conv_w8, conv_b, alog, ef, eb)


def _bwd_sweep_kernel(n_blk, cur_ref, prev_ref, next_ref, dt_ref, smat_ref, cw_ref, cb_ref, alog_ref, eb_ref,
                      sb0_ref, act_ref, sin_ref, buf_ref, p_ref, state_ref):
    j = pl.program_id(1)
    blk = n_blk - 1 - j
    rows = SSD_ROWS
    n_chunks = rows // CHUNK

    @pl.when(j == 0)
    def _():
        state_ref[...] = sb0_ref[...]

    zeros = jnp.zeros((BF16_ROWS, CONV_CH), BF16)
    buf_ref[0:BF16_ROWS, :] = jnp.where(blk > 0, prev_ref[...], zeros)
    buf_ref[pl.ds(BF16_ROWS, rows), :] = cur_ref[...]
    buf_ref[pl.ds(BF16_ROWS + rows, BF16_ROWS), :] = jnp.where(blk < n_blk - 1, next_ref[...], zeros)
    _conv_silu(buf_ref, n_chunks, smat_ref, cw_ref, cb_ref, p_ref, act_ref)

    a_row = _neg_a_log2(alog_ref)
    _, upper = _tri()
    dts = [dt_ref[pl.ds(c * CHUNK, CHUNK), :] for c in range(n_chunks)]
    acs_b = _cumsum_rows([d * a_row for d in dts], upper.astype(BF16))
    w_exp, tot_exp = _state_weights(acs_b, dts, eb_ref, 0)
    for c in reversed(range(n_chunks)):
        sin_ref[c] = state_ref[...].astype(BF16)
        _chunk_state_update(state_ref, act_ref, c, w_exp, tot_exp)


def _bwd_sweep(xbc, dt, smat, conv_w8, conv_b, alog, eb, sb0, batch, seq_len):
    t = xbc.shape[0]
    rows = SSD_ROWS
    n_blk = seq_len // rows
    cpb = rows // CHUNK
    halo_per_blk = rows // BF16_ROWS
    n_halo = t // BF16_ROWS

    def blk_of(b, j):
        return b * n_blk + (n_blk - 1 - j)

    return pl.pallas_call(
        functools.partial(_bwd_sweep_kernel, n_blk),
        grid=(batch, n_blk),
        in_specs=[pl.BlockSpec((rows, CONV_CH), lambda b, j: (blk_of(b, j), 0)),
                  pl.BlockSpec((BF16_ROWS, CONV_CH),
                               lambda b, j: (jnp.maximum(blk_of(b, j) * halo_per_blk - 1, 0), 0)),
                  pl.BlockSpec((BF16_ROWS, CONV_CH),
                               lambda b, j: (jnp.minimum((blk_of(b, j) + 1) * halo_per_blk, n_halo - 1), 0)),
                  pl.BlockSpec((rows, DT_W), lambda b, j: (blk_of(b, j), 0)),
                  _const_spec(smat.shape), _const_spec((SUBLANES, CONV_CH)), _const_spec((1, CONV_CH)),
                  _const_spec((1, LANES)), _const_spec(eb.shape),
                  pl.BlockSpec((None, N_SSM_GROUPS, D_STATE, GROUP_W), lambda b, j: (b, 0, 0, 0))],
        out_specs=[pl.BlockSpec((rows, CONV_CH), lambda b, j: (blk_of(b, j), 0)),
                   pl.BlockSpec((cpb, N_SSM_GROUPS, D_STATE, GROUP_W), lambda b, j: (blk_of(b, j), 0, 0, 0))],
        out_shape=[jax.ShapeDtypeStruct((t, CONV_CH), BF16),
                   jax.ShapeDtypeStruct((t // CHUNK, N_SSM_GROUPS, D_STATE, GROUP_W), BF16)],
        scratch_shapes=_conv_scratch(rows) + [pltpu.VMEM((N_SSM_GROUPS, D_STATE, GROUP_W), F32)],
        compiler_params=_cparams(("parallel", "arbitrary")),
        name="ssd_backward_sweep",
    )(xbc, xbc, xbc, dt, smat, conv_w8, conv_b, alog, eb, sb0)


def _fwd_sweep_kernel(act_ref, dt_ref, z_ref, sbin_ref, sf0_ref, alog_ref, ef2_ref, dskip_ref, ng_ref,
                      y_ref, state_ref, ybuf_ref):
    j = pl.program_id(1)

    @pl.when(j == 0)
    def _():
        state_ref[...] = sf0_ref[...]

    a_row = _neg_a_log2(alog_ref)
    lower, upper = _tri()
    lower_bf16 = lower.astype(BF16)
    upper_bf16 = upper.astype(BF16)
    diag = lower & upper
    lane = lax.broadcasted_iota(jnp.int32, (1, LANES), 1)
    lane_lo = lane < SSM_HEAD_DIM
    heads_per_group = N_SSM_HEADS // N_SSM_GROUPS

    n_chunks = SSD_ROWS // CHUNK
    dts = [dt_ref[pl.ds(c * CHUNK, CHUNK), :] for c in range(n_chunks)]
    adts = [d * a_row for d in dts]
    acs_f_all = _cumsum_rows(adts, lower_bf16)
    acs_b_all = _cumsum_rows(adts, upper_bf16)
    adt_ts = [a.T for a in adts]
    acst_f_all = _cumsum_cols(adt_ts, upper_bf16)
    acst_b_all = _cumsum_cols(adt_ts, lower_bf16)
    w_all, tot_all = _state_weights(acs_f_all, dts, ef2_ref, CHUNK - 1)

    for c in range(n_chunks):
        rows = pl.ds(c * CHUNK, CHUNK)
        crows = slice(c * CHUNK, (c + 1) * CHUNK)
        acs_f, acs_b = acs_f_all[c], acs_b_all[c]
        dt_t = dts[c].T
        log2_dt_t = jnp.log2(dt_t)
        row_f = acst_f_all[c] - log2_dt_t
        row_b = acst_b_all[c] - log2_dt_t
        diag_log = jnp.log2(dt_t[0:N_SSM_HEADS] + dt_t[N_SSM_HEADS:2 * N_SSM_HEADS])

        b_chunk = act_ref[rows, D_INNER:D_INNER + BC_W]
        c_chunk = act_ref[rows, D_INNER + BC_W:CONV_CH]

        for g in range(N_SSM_GROUPS):
            c_g = c_chunk[:, g * D_STATE:(g + 1) * D_STATE]
            cb = _dot_nt(c_g, b_chunk[:, g * D_STATE:(g + 1) * D_STATE])
            c_g32 = c_g.astype(F32)
            for pair in range(g * heads_per_group // 2, (g + 1) * heads_per_group // 2):
                pcols = slice((pair % (heads_per_group // 2)) * LANES, (pair % (heads_per_group // 2) + 1) * LANES)
                rhs = jnp.concatenate([act_ref[rows, pair * LANES:(pair + 1) * LANES],
                                       state_ref[g, :, pcols].astype(BF16), sbin_ref[c, g, :, pcols]], axis=0)
                outs = []
                for h in (2 * pair, 2 * pair + 1):
                    hb = N_SSM_HEADS + h
                    col_f = jnp.broadcast_to(acs_f[:, h:h + 1], (CHUNK, CHUNK))
                    col_b = jnp.broadcast_to(acs_b[:, hb:hb + 1], (CHUNK, CHUNK))
                    seg = jnp.where(diag, diag_log[h:h + 1, :],
                                    jnp.where(lower, col_f - row_f[h:h + 1, :], col_b - row_b[hb:hb + 1, :]))
                    lhs = jnp.concatenate([(cb * jnp.exp2(seg)).astype(BF16),
                                           (c_g32 * jnp.exp2(col_f)).astype(BF16),
                                           (c_g32 * jnp.exp2(col_b)).astype(BF16)], axis=1)
                    outs.append(_dot(lhs, rhs))
                ybuf_ref[:, pair * LANES:(pair + 1) * LANES] = jnp.where(lane_lo, outs[0], outs[1])

        for g in range(N_SSM_GROUPS):
            cols = slice(g * GROUP_W, (g + 1) * GROUP_W)
            y = ybuf_ref[:, cols] + dskip_ref[:, cols] * act_ref[rows, cols].astype(F32)
            y = y * z_ref[rows, cols].astype(F32)
            ms = jnp.mean(y * y, axis=-1, keepdims=True)
            y_ref[rows, cols] = (y * lax.rsqrt(ms + EPS) * ng_ref[:, cols]).astype(BF16)

        _state_update(state_ref, b_chunk, act_ref[rows, 0:D_INNER], w_all[crows], tot_all[c:c + 1])


def _fwd_sweep(act, dt, z, sb_in, sf0, alog, ef2, dskip, norm_g, batch, seq_len):
    t = act.shape[0]
    rows = SSD_ROWS
    n_blk = seq_len // rows
    cpb = rows // CHUNK
    blk = lambda b, j: b * n_blk + j
    return pl.pallas_call(
        _fwd_sweep_kernel,
        grid=(batch, n_blk),
        in_specs=[pl.BlockSpec((rows, CONV_CH), lambda b, j: (blk(b, j), 0)),
                  pl.BlockSpec((rows, DT_W), lambda b, j: (blk(b, j), 0)),
                  pl.BlockSpec((rows, D_INNER), lambda b, j: (blk(b, j), 0)),
                  pl.BlockSpec((cpb, N_SSM_GROUPS, D_STATE, GROUP_W), lambda b, j: (blk(b, j), 0, 0, 0)),
                  pl.BlockSpec((None, N_SSM_GROUPS, D_STATE, GROUP_W), lambda b, j: (b, 0, 0, 0)),
                  _const_spec((1, LANES)), _const_spec(ef2.shape),
                  _const_spec((1, D_INNER)), _const_spec((1, D_INNER))],
        out_specs=pl.BlockSpec((rows, D_INNER), lambda b, j: (blk(b, j), 0)),
        out_shape=jax.ShapeDtypeStruct((t, D_INNER), BF16),
        scratch_shapes=[pltpu.VMEM((N_SSM_GROUPS, D_STATE, GROUP_W), F32),
                        pltpu.VMEM((CHUNK, D_INNER), F32)],
        compiler_params=_cparams(("parallel", "arbitrary")),
        name="ssd_forward_sweep",
    )(act, dt, z, sb_in, sf0, alog, ef2, dskip, norm_g)


FF_SPLIT = 1280


def _merge_ffn_kernel(x_ref, ya_ref, ys_ref, ga_ref, gb_ref, g1_ref, sc_ref, sh_ref, g2_ref, ng_ref, fg_ref,
                      woa_ref, wob_ref, wout_ref, win_ref, wffo_ref, o_ref):
    half_rows = ROW_TILE // 2
    halves = [slice(k * half_rows, (k + 1) * half_rows) for k in range(2)]

    def merge(r):
        a = _dot(ya_ref[r, :], woa_ref[...])
        b = _dot(ys_ref[r, :], wob_ref[...])
        merged = (ga_ref[r, :].astype(F32) * a + gb_ref[r, :].astype(F32) * b).astype(BF16)
        return x_ref[r, :] + g1_ref[...] * _dot(merged, wout_ref[...])

    def swiglu(h):
        acc = None
        for c0, c1 in ((0, FF_SPLIT), (FF_SPLIT, D_FF)):
            gate = _dot(h, win_ref[:, c0:c1])
            up = _dot(h, win_ref[:, D_FF + c0:D_FF + c1])
            part = _dot((_silu(gate) * up).astype(BF16), wffo_ref[c0:c1, :])
            acc = part if acc is None else acc + part
        return acc

    x1 = [merge(r) for r in halves]
    hs = [_adaln(v, ng_ref, sc_ref, sh_ref).astype(BF16) for v in x1]
    ff = [swiglu(h) for h in hs]
    for r, v, f in zip(halves, x1, ff):
        x2 = v + g2_ref[...] * f
        ms = jnp.mean(x2 * x2, axis=-1, keepdims=True)
        o_ref[r, :] = x2 * lax.rsqrt(ms + EPS) * fg_ref[...]


def _merge_ffn(x2, y_att, y_ssm, ga, gb, g1, sc2, sh2, g2, norm_g, final_g, w_oa, w_ob, w_out, w_ffn_in, w_ffn_out,
               seq_len):
    t, d = x2.shape
    tm = ROW_TILE
    per_seq = seq_len // tm
    row = lambda w: pl.BlockSpec((tm, w), lambda i: (i, 0))
    mod_spec = pl.BlockSpec((None, 1, d), lambda i: (i // per_seq, 0, 0))
    return pl.pallas_call(
        _merge_ffn_kernel,
        grid=(t // tm,),
        in_specs=[row(d), row(ATT_W), row(D_INNER), row(d), row(d), mod_spec, mod_spec, mod_spec, mod_spec,
                  _const_spec((1, d)), _const_spec((1, d)),
                  _const_spec(w_oa.shape), _const_spec(w_ob.shape), _const_spec(w_out.shape),
                  _const_spec(w_ffn_in.shape), _const_spec(w_ffn_out.shape)],
        out_specs=row(d),
        out_shape=jax.ShapeDtypeStruct((t, d), F32),
        compiler_params=_cparams(("parallel",)),
        name="merge_ffn_final_norm",
    )(x2, y_att, y_ssm, ga, gb, g1, sc2, sh2, g2, norm_g, final_g, w_oa, w_ob, w_out, w_ffn_in, w_ffn_out)


def _rope_tables(seq_len):
    rows = seq_len // GRID_W
    row = jnp.repeat(jnp.arange(rows), GRID_W).astype(F32)
    col = jnp.tile(jnp.arange(GRID_W), rows).astype(F32)
    inv = ROPE_BASE ** (-jnp.arange(ROPE_FREQS, dtype=F32) / ROPE_FREQS)
    ang_r = row[:, None] * inv
    ang_c = col[:, None] * inv
    ang = jnp.concatenate([ang_r, ang_r, ang_c, ang_c] * 2, axis=-1)
    sign = np.where((np.arange(LANES) % (2 * ROPE_FREQS)) < ROPE_FREQS, -1.0, 1.0).astype(np.float32)
    return jnp.cos(ang), jnp.sin(ang) * jnp.asarray(sign)[None, :]


def _expand_matrix(first_row):
    e = np.zeros((LANES, D_INNER), np.float32)
    for h in range(N_SSM_HEADS):
        e[first_row + h, h * SSM_HEAD_DIM:(h + 1) * SSM_HEAD_DIM] = 1.0
    return jnp.asarray(e, dtype=BF16)


def _conv_shift_matrix():
    s = np.zeros((CHUNK, CONV_K * CONV_WIN), np.float32)
    for k in range(CONV_K):
        off = k + BF16_ROWS - CONV_K // 2 if k <= CONV_K // 2 else k - CONV_K // 2
        for t in range(CHUNK):
            s[t, k * CONV_WIN + t + off] = 1.0
    return jnp.asarray(s, dtype=BF16)


def kernel(x, c, ctx, c_ctx, w_mod, b_mod, norm1_g, w_in, attn_sink, conv_w, conv_b, a_log_f, a_log_b,
           dt_bias_f, dt_bias_b, d_skip, ssm_norm_g, w_oa, w_ob, w_out, norm2_g, w_ffn_in, w_ffn_out, final_g):
    batch, seq_len, d = x.shape
    ctx_len = ctx.shape[1]
    assert w_mod.shape[0] == 1 and d == D_MODEL
    assert seq_len % SSD_ROWS == 0 and seq_len % ROW_TILE == 0 and seq_len % GRID_W == 0
    assert ctx_len % CHUNK == 0

    pad_rows = (-(batch + 1)) % SUBLANES
    c_all = jnp.concatenate([c, c_ctx[None, :], jnp.zeros((pad_rows, d), F32)], axis=0)
    layer = lambda p: p.reshape(p.shape[1:])
    mod = _modulation(c_all, layer(w_mod), b_mod[0][None, :])
    mod_b = mod[:batch].reshape(batch, N_MOD, 1, d)
    sh1, sc1, g1, sh2, sc2, g2 = (mod_b[:, k] for k in range(N_MOD))
    mod_c = mod[batch:batch + 1].reshape(1, N_MOD, 1, d)
    sh1c, sc1c = mod_c[:, 0], mod_c[:, 1]

    w = layer(w_in)
    o = np.cumsum([0, ATT_W, KV_W, KV_W, D_INNER, CONV_CH, N_SSM_HEADS, N_SSM_HEADS, D_MODEL, D_MODEL])
    seg = lambda k: w[:, o[k]:o[k + 1]]
    w_main = jnp.concatenate(
        [seg(0), seg(1), seg(2), seg(3), seg(4), seg(7), seg(8), seg(5), seg(6),
         jnp.zeros((d, DT_W - 2 * N_SSM_HEADS), F32)], axis=1).astype(BF16)
    pad_heads = jnp.zeros((DT_W - 2 * N_SSM_HEADS,), F32)
    dt_bias = jnp.concatenate([dt_bias_f[0], dt_bias_b[0], pad_heads])[None, :]
    alog = jnp.concatenate([a_log_f[0], a_log_b[0], pad_heads])[None, :]
    conv_w8 = jnp.concatenate([conv_w[0], jnp.zeros((SUBLANES - CONV_K, CONV_CH), F32)], axis=0)
    conv_b1 = conv_b[0][None, :]
    cos, sin = _rope_tables(seq_len)
    ef2 = _expand_matrix(0)
    eb2 = _expand_matrix(N_SSM_HEADS)
    g_norm1 = norm1_g[0][None, :]

    x2 = x.reshape(batch * seq_len, d)
    xc2 = ctx.reshape(batch * ctx_len, d)

    kv_c, xbc_c, dt_c = _inproj_context(xc2, sc1c, sh1c, g_norm1, w_main, dt_bias, ctx_len)
    z, xbc, dt, ga, gb, y_att = _inproj_attention(x2, sc1, sh1, g_norm1, w_main, dt_bias, cos, sin, kv_c,
                                                  attn_sink[0], seq_len, ctx_len)

    smat = _conv_shift_matrix()
    sf0, sb0 = _ctx_states(xbc_c, dt_c, smat, conv_w8, conv_b1, alog, ef2, eb2, batch, ctx_len)
    act, sb_in = _bwd_sweep(xbc, dt, smat, conv_w8, conv_b1, alog, eb2, sb0, batch, seq_len)
    dskip = jnp.repeat(d_skip[0], SSM_HEAD_DIM)[None, :]
    y_ssm = _fwd_sweep(act, dt, z, sb_in, sf0, alog, ef2, dskip, ssm_norm_g[0][None, :], batch, seq_len)

    out = _merge_ffn(x2, y_att, y_ssm, ga, gb, g1, sc2, sh2, g2, norm2_g[0][None, :], final_g[None, :],
                     layer(w_oa).astype(BF16), layer(w_ob).astype(BF16), layer(w_out).astype(BF16),
                     layer(w_ffn_in).astype(BF16), layer(w_ffn_out).astype(BF16), seq_len)
    return out.reshape(batch, seq_len, d)
```

```python
import functools

import numpy as np
import jax
import jax.numpy as jnp
from jax import lax
from jax.experimental import pallas as pl
from jax.experimental.pallas import tpu as pltpu

F32 = jnp.float32
BF16 = jnp.bfloat16

D_MODEL = 1024
GRID_W = 64
EPS = 1e-6
HEAD_DIM = 64
N_Q_HEADS = 16
N_KV_HEADS = 4
ATT_W = N_Q_HEADS * HEAD_DIM
KV_W = N_KV_HEADS * HEAD_DIM
WINDOW = 128
ROPE_BASE = 10000.0
ROPE_FREQS = HEAD_DIM // 4
D_INNER = 2 * D_MODEL
SSM_HEAD_DIM = 64
N_SSM_HEADS = D_INNER // SSM_HEAD_DIM
N_SSM_GROUPS = 4
GROUP_W = D_INNER // N_SSM_GROUPS
D_STATE = 128
BC_W = N_SSM_GROUPS * D_STATE
CONV_K = 5
CONV_CH = D_INNER + 2 * BC_W
CHUNK = 128
CONV_WIN = CHUNK + 16
D_FF = -(-8 * D_MODEL // (3 * 256)) * 256
N_MOD = 6

LANES = 128
SUBLANES = 8
BF16_ROWS = 16
VMEM_LIMIT = 56 * 1024 * 1024

OFF_Q = 0
OFF_K = OFF_Q + ATT_W
OFF_V = OFF_K + KV_W
OFF_Z = OFF_V + KV_W
OFF_XBC = OFF_Z + D_INNER
OFF_GA = OFF_XBC + CONV_CH
OFF_GB = OFF_GA + D_MODEL
OFF_DT = OFF_GB + D_MODEL
DT_W = LANES
N_PROJ = OFF_DT + DT_W
KV_CAT = 2 * N_KV_HEADS * LANES
V_OFF = N_KV_HEADS * LANES
LOG2E = 1.4426950408889634

ROW_TILE = 512
SSD_ROWS = 512
ATT_BLOCK = 128


def _cparams(sem):
    return pltpu.CompilerParams(dimension_semantics=sem, vmem_limit_bytes=VMEM_LIMIT)


def _const_spec(shape):
    n = len(shape)
    return pl.BlockSpec(shape, lambda *_: (0,) * n, pipeline_mode=pl.Buffered(1))


def _dot(a, b):
    return jnp.dot(a, b, preferred_element_type=F32)


def _dot_nt(a, b):
    return lax.dot_general(a, b, (((1,), (1,)), ((), ())), preferred_element_type=F32)


def _dot_tn(a, b):
    return lax.dot_general(a, b, (((0,), (0,)), ((), ())), preferred_element_type=F32)


def _sigmoid(v):
    return 1.0 / (1.0 + jnp.exp(-v))


def _silu(v):
    return v * _sigmoid(v)


def _split3(v):
    hi = v.astype(BF16)
    r = v - hi.astype(F32)
    mid = r.astype(BF16)
    lo = (r - mid.astype(F32)).astype(BF16)
    return hi, mid, lo


def _mod_kernel(c_ref, w_ref, b_ref, o_ref):
    act = _silu(c_ref[...])
    o_ref[...] = jnp.dot(act, w_ref[...], preferred_element_type=F32,
                         precision=lax.Precision.HIGHEST) + b_ref[...]


def _modulation(c_all, w_mod, b_mod):
    rows, d = c_all.shape
    n = w_mod.shape[1]
    tn = 1024
    return pl.pallas_call(
        _mod_kernel,
        grid=(n // tn,),
        in_specs=[pl.BlockSpec((rows, d), lambda j: (0, 0)),
                  pl.BlockSpec((d, tn), lambda j: (0, j)),
                  pl.BlockSpec((1, tn), lambda j: (0, j))],
        out_specs=pl.BlockSpec((rows, tn), lambda j: (0, j)),
        out_shape=jax.ShapeDtypeStruct((rows, n), F32),
        compiler_params=_cparams(("arbitrary",)),
        name="modulation",
    )(c_all, w_mod, b_mod)


def _adaln(x, g_ref, sc_ref, sh_ref):
    ms = jnp.mean(x * x, axis=-1, keepdims=True)
    xn = x * lax.rsqrt(ms + EPS)
    return xn * (g_ref[...] * (1.0 + sc_ref[...])) + sh_ref[...]


def _store_padded_heads(val, col0, kv_ref):
    lane = lax.broadcasted_iota(jnp.int32, (1, LANES), 1)
    is_lo = lane < HEAD_DIM
    for s in range(2):
        slab = val[:, s * LANES:(s + 1) * LANES]
        swapped = pltpu.roll(slab, HEAD_DIM, 1)
        for h, src in ((2 * s, slab), (2 * s + 1, swapped)):
            kv_ref[:, col0 + h * LANES: col0 + (h + 1) * LANES] = jnp.where(is_lo, src, 0.0).astype(BF16)


def _rope_slab(slab, cos, sin_signed, first_half):
    fwd = pltpu.roll(slab, LANES - ROPE_FREQS, 1)
    bwd = pltpu.roll(slab, ROPE_FREQS, 1)
    return slab * cos + jnp.where(first_half, fwd, bwd) * sin_signed


PROJ_PIECE = 1024


def _softplus(v):
    return jnp.maximum(v, 0.0) + jnp.log1p(jnp.exp(-jnp.abs(v)))


def _project_kv(proj, rope, kv_ref):
    k = proj(OFF_K, KV_W)
    if rope is not None:
        k = jnp.concatenate([_rope_slab(k[:, s * LANES:(s + 1) * LANES], *rope) for s in range(2)], axis=1)
    _store_padded_heads(k, 0, kv_ref)
    _store_padded_heads(proj(OFF_V, KV_W), V_OFF, kv_ref)


def _inproj_ctx_kernel(x_ref, sc_ref, sh_ref, g_ref, w_ref, dtb_ref, kv_ref, xbc_ref, dt_ref):
    h = _adaln(x_ref[...], g_ref, sc_ref, sh_ref).astype(BF16)
    proj = lambda off, width: _dot(h, w_ref[:, off:off + width])
    _project_kv(proj, None, kv_ref)
    for c0 in range(0, CONV_CH, PROJ_PIECE):
        xbc_ref[:, c0:c0 + PROJ_PIECE] = proj(OFF_XBC + c0, PROJ_PIECE).astype(BF16)
    dt_ref[...] = _softplus(proj(OFF_DT, DT_W) + dtb_ref[...])


def _inproj_context(xc2, sc, sh, g, w_main, dt_bias, ctx_len):
    t, d = xc2.shape
    tm = ctx_len
    mod_spec = pl.BlockSpec((None, 1, d), lambda i: (0, 0, 0))
    row = lambda w: pl.BlockSpec((tm, w), lambda i: (i, 0))
    sds = lambda w, dt: jax.ShapeDtypeStruct((t, w), dt)
    return pl.pallas_call(
        _inproj_ctx_kernel,
        grid=(t // tm,),
        in_specs=[row(d), mod_spec, mod_spec, _const_spec((1, d)), _const_spec((d, N_PROJ)),
                  _const_spec((1, DT_W))],
        out_specs=[row(KV_CAT), row(CONV_CH), row(DT_W)],
        out_shape=[sds(KV_CAT, BF16), sds(CONV_CH, BF16), sds(DT_W, F32)],
        compiler_params=_cparams(("parallel",)),
        name="inproj_context",
    )(xc2, sc, sh, g, w_main, dt_bias)


def _attn_scores(q_slabs, k_parts):
    half = HEAD_DIM

    def swap_halves(t):
        return jnp.concatenate([t[half:], t[:half]], axis=0)

    t0, t1 = (q.astype(F32).T.astype(BF16) for q in q_slabs)
    q_t = jnp.concatenate([t0, swap_halves(t0), t1, swap_halves(t1)], axis=1)
    return _dot(jnp.concatenate(k_parts, axis=0), q_t)


def _attn_finish(s, v_parts, bias_prev, bias_next, sink_row):
    half = HEAD_DIM
    s_p = s[0:ATT_BLOCK] + bias_prev
    s_c = s[ATT_BLOCK:2 * ATT_BLOCK]
    s_n = s[2 * ATT_BLOCK:3 * ATT_BLOCK] + bias_next
    s_x = s[3 * ATT_BLOCK:]
    red = lambda f, parts: functools.reduce(f, parts)
    m = red(jnp.maximum, [jnp.max(p, axis=0, keepdims=True) for p in (s_p, s_c, s_n, s_x)])
    m = jnp.maximum(m, sink_row)
    e = [jnp.exp2(p - m) for p in (s_p, s_c, s_n, s_x)]
    denom = red(jnp.add, [jnp.sum(p, axis=0, keepdims=True) for p in e]) + jnp.exp2(sink_row - m)
    p_all = jnp.concatenate([p.astype(BF16) for p in e], axis=0)
    v_t = jnp.concatenate(v_parts, axis=0).T[:half]
    out_t = _dot(v_t, p_all) * (1.0 / denom)
    return [jnp.concatenate([out_t[:, (2 * sl) * ATT_BLOCK:(2 * sl + 1) * ATT_BLOCK],
                             out_t[:, (2 * sl + 1) * ATT_BLOCK:(2 * sl + 2) * ATT_BLOCK]], axis=0).T
            for sl in range(2)]


def _inproj_attn_kernel(per_seq, sink_ref, x_ref, sc_ref, sh_ref, g_ref, w_ref, dtb_ref, cos_ref, sin_ref, kvx_ref,
                        z_ref, xbc_ref, dt_ref, ga_ref, gb_ref, ya_ref,
                        q_cur, q_nxt, kv_cur, kv_nxt, kv_tail):
    i = pl.program_id(0)
    blocks = ROW_TILE // ATT_BLOCK
    group = N_Q_HEADS // N_KV_HEADS
    width = group * ATT_BLOCK

    @pl.when(i == 0)
    def _():
        q_cur[...] = jnp.zeros(q_cur.shape, BF16)
        kv_cur[...] = jnp.zeros(kv_cur.shape, BF16)
        kv_tail[...] = jnp.zeros(kv_tail.shape, BF16)

    h = _adaln(x_ref[...], g_ref, sc_ref, sh_ref).astype(BF16)
    proj = lambda off, wid: _dot(h, w_ref[:, off:off + wid])
    cos = cos_ref[...]
    sin = sin_ref[...]
    lane = lax.broadcasted_iota(jnp.int32, (1, LANES), 1)
    first_half = (lane % (2 * ROPE_FREQS)) < ROPE_FREQS
    _project_kv(proj, (cos, sin, first_half), kv_nxt)
    scale = HEAD_DIM ** -0.5 * LOG2E
    q = proj(OFF_Q, ATT_W)
    for s in range(ATT_W // LANES):
        q_nxt[:, s * LANES:(s + 1) * LANES] = (
            _rope_slab(q[:, s * LANES:(s + 1) * LANES], cos, sin, first_half) * scale).astype(BF16)

    piece = PROJ_PIECE // 2

    def store_xbc(c0):
        xbc_ref[:, c0:c0 + piece] = proj(OFF_XBC + c0, piece).astype(BF16)

    def store_z(c0):
        z_ref[:, c0:c0 + piece] = _silu(proj(OFF_Z + c0, piece)).astype(BF16)

    def store_dt():
        dt_ref[...] = _softplus(proj(OFF_DT, DT_W) + dtb_ref[...])

    def store_gate(ref, off, c0):
        ref[:, c0:c0 + piece] = _sigmoid(proj(off + c0, piece)).astype(BF16)

    segments = ([functools.partial(store_xbc, c0) for c0 in range(0, CONV_CH, piece)] + [store_dt]
                + [functools.partial(store_z, c0) for c0 in range(0, D_INNER, piece)]
                + [functools.partial(store_gate, ga_ref, OFF_GA, c0) for c0 in range(0, D_MODEL, piece)]
                + [functools.partial(store_gate, gb_ref, OFF_GB, c0) for c0 in range(0, D_MODEL, piece)])

    seq_tile = (i + per_seq - 1) % per_seq
    key_j = lax.broadcasted_iota(jnp.int32, (ATT_BLOCK, width), 0)
    lane_w = lax.broadcasted_iota(jnp.int32, (ATT_BLOCK, width), 1)
    query_r = lane_w % ATT_BLOCK
    neg = jnp.float32(-1e30)
    band_prev = key_j >= query_r
    band_next = key_j <= query_r
    lane1 = lax.broadcasted_iota(jnp.int32, (1, width), 1)

    def parts(sub, kvh, col0):
        qrows = slice(sub * ATT_BLOCK, (sub + 1) * ATT_BLOCK)
        cols = slice(col0 + kvh * LANES, col0 + (kvh + 1) * LANES)
        prev = kv_tail[:, cols] if sub == 0 else kv_cur[(sub - 1) * ATT_BLOCK:sub * ATT_BLOCK, cols]
        nxt = (kv_nxt[0:ATT_BLOCK, cols] if sub == blocks - 1
               else kv_cur[(sub + 1) * ATT_BLOCK:(sub + 2) * ATT_BLOCK, cols])
        return [prev, kv_cur[qrows, cols], nxt, kvx_ref[:, cols]]

    def scores(sub, kvh):
        qrows = slice(sub * ATT_BLOCK, (sub + 1) * ATT_BLOCK)
        q_slabs = [q_cur[qrows, (2 * kvh + sl) * LANES:(2 * kvh + sl + 1) * LANES] for sl in range(2)]
        return _attn_scores(q_slabs, parts(sub, kvh, 0))

    work = [(sub, kvh) for sub in range(blocks) for kvh in range(N_KV_HEADS)]
    s_next = scores(*work[0])
    for n, (sub, kvh) in enumerate(work):
        s_cur = s_next
        if n + 1 < len(work):
            s_next = scores(*work[n + 1])
        if segments:
            segments.pop(0)()
        qrows = slice(sub * ATT_BLOCK, (sub + 1) * ATT_BLOCK)
        prev_ok = (seq_tile > 0) if sub == 0 else True
        next_ok = (seq_tile < per_seq - 1) if sub == blocks - 1 else True
        bias_prev = jnp.where(band_prev & prev_ok, 0.0, neg)
        bias_next = jnp.where(band_next & next_ok, 0.0, neg)
        sink_row = jnp.zeros((1, width), F32)
        for g in range(group):
            sink_row = jnp.where(lane1 // ATT_BLOCK == g, sink_ref[kvh * group + g] * LOG2E, sink_row)
        outs = _attn_finish(s_cur, parts(sub, kvh, V_OFF), bias_prev, bias_next, sink_row)
        for sl in range(2):
            ya_ref[qrows, (2 * kvh + sl) * LANES:(2 * kvh + sl + 1) * LANES] = outs[sl].astype(BF16)
    for seg in segments:
        seg()

    kv_tail[...] = kv_cur[ROW_TILE - ATT_BLOCK:, :]
    kv_cur[...] = kv_nxt[...]
    q_cur[...] = q_nxt[...]


def _inproj_attention(x2, sc, sh, g, w_main, dt_bias, cos, sin, kv_ctx, sink, seq_len, ctx_len):
    t, d = x2.shape
    tm = ROW_TILE
    per_seq = seq_len // tm
    n_tiles = t // tm
    cur = lambda i: jnp.minimum(i, n_tiles - 1)
    prev = lambda i: jnp.maximum(i - 1, 0)
    mod_spec = pl.BlockSpec((None, 1, d), lambda i: (cur(i) // per_seq, 0, 0))
    pos_spec = pl.BlockSpec((tm, LANES), lambda i: (cur(i) % per_seq, 0))
    row = lambda w: pl.BlockSpec((tm, w), lambda i: (cur(i), 0))
    sds = lambda w, dt: jax.ShapeDtypeStruct((t, w), dt)
    return pl.pallas_call(
        functools.partial(_inproj_attn_kernel, per_seq),
        grid=(n_tiles + 1,),
        in_specs=[pl.BlockSpec(memory_space=pltpu.SMEM), row(d), mod_spec, mod_spec, _const_spec((1, d)),
                  _const_spec((d, N_PROJ)), _const_spec((1, DT_W)), pos_spec, pos_spec,
                  pl.BlockSpec((ctx_len, KV_CAT), lambda i: (prev(i) // per_seq, 0))],
        out_specs=[row(D_INNER), row(CONV_CH), row(DT_W), row(d), row(d),
                   pl.BlockSpec((tm, ATT_W), lambda i: (prev(i), 0))],
        out_shape=[sds(D_INNER, BF16), sds(CONV_CH, BF16), sds(DT_W, F32), sds(d, BF16), sds(d, BF16),
                   sds(ATT_W, BF16)],
        scratch_shapes=[pltpu.VMEM((tm, ATT_W), BF16), pltpu.VMEM((tm, ATT_W), BF16),
                        pltpu.VMEM((tm, KV_CAT), BF16), pltpu.VMEM((tm, KV_CAT), BF16),
                        pltpu.VMEM((ATT_BLOCK, KV_CAT), BF16)],
        compiler_params=_cparams(("arbitrary",)),
        name="inproj_attention",
    )(sink, x2, sc, sh, g, w_main, dt_bias, cos, sin, kv_ctx)


def _conv_silu(buf_ref, n_blocks, smat_ref, w_ref, b_ref, p_ref, out_ref):
    for m in range(n_blocks):
        r0 = m * CHUNK
        slot = m % 2
        for k in range(CONV_K):
            start = r0 if k <= CONV_K // 2 else r0 + BF16_ROWS
            p_ref[slot, k * CONV_WIN:(k + 1) * CONV_WIN, :] = (
                buf_ref[pl.ds(start, CONV_WIN), :] * w_ref[k:k + 1, :].astype(BF16))
        acc = _dot(smat_ref[...], p_ref[slot]) + b_ref[...]
        out_ref[pl.ds(r0, CHUNK), :] = _silu(acc).astype(out_ref.dtype)


def _tri():
    r = lax.broadcasted_iota(jnp.int32, (CHUNK, CHUNK), 0)
    c = lax.broadcasted_iota(jnp.int32, (CHUNK, CHUNK), 1)
    return r >= c, r <= c


def _cumsum_rows(adts, tri_bf16):
    x3 = jnp.concatenate([t for a in adts for t in _split3(a)], axis=1)
    c = _dot(tri_bf16, x3)
    w = 3 * LANES
    return [c[:, i * w:i * w + LANES] + c[:, i * w + LANES:i * w + 2 * LANES] + c[:, i * w + 2 * LANES:(i + 1) * w]
            for i in range(len(adts))]


def _cumsum_cols(adt_ts, tri_bf16):
    x3 = jnp.concatenate([t for a in adt_ts for t in _split3(a)], axis=0)
    c = _dot(x3, tri_bf16)
    h = 3 * CHUNK
    return [c[i * h:i * h + CHUNK] + c[i * h + CHUNK:i * h + 2 * CHUNK] + c[i * h + 2 * CHUNK:(i + 1) * h]
            for i in range(len(adt_ts))]


def _expand(v, e_ref):
    return _dot(v.astype(BF16), e_ref[...])


def _state_update(state_ref, b_chunk, xs_chunk, w_exp, tot_exp):
    for g in range(N_SSM_GROUPS):
        cols = slice(g * GROUP_W, (g + 1) * GROUP_W)
        xw = (xs_chunk[:, cols].astype(F32) * w_exp[:, cols]).astype(BF16)
        upd = _dot_tn(b_chunk[:, g * D_STATE:(g + 1) * D_STATE], xw)
        state_ref[g] = state_ref[g] * tot_exp[:, cols] + upd


def _neg_a_log2(alog_ref):
    return -jnp.exp(alog_ref[...]) * LOG2E


def _state_weights(acs, dts, e_ref, total_row):
    totals = [a[total_row:total_row + 1, :] for a in acs]
    w = jnp.concatenate([jnp.exp2(t - a) * d for t, a, d in zip(totals, acs, dts)], axis=0)
    pad = jnp.zeros((SUBLANES - len(acs), LANES), F32)
    tot = jnp.concatenate([jnp.exp2(t) for t in totals] + [pad], axis=0)
    return _expand(w, e_ref), _expand(tot, e_ref)


def _chunk_state_update(state_ref, act_ref, c, w_exp, tot_exp):
    rows = pl.ds(c * CHUNK, CHUNK)
    _state_update(state_ref, act_ref[rows, D_INNER:D_INNER + BC_W], act_ref[rows, 0:D_INNER],
                  w_exp[c * CHUNK:(c + 1) * CHUNK], tot_exp[c:c + 1])


def _ctx_state_kernel(n_chunks, xbc_ref, dt_ref, smat_ref, cw_ref, cb_ref, alog_ref, ef_ref, eb_ref,
                      sf_ref, sb_ref, buf_ref, p_ref, act_ref):
    rows = n_chunks * CHUNK
    zeros = jnp.zeros((BF16_ROWS, CONV_CH), BF16)
    buf_ref[0:BF16_ROWS, :] = zeros
    buf_ref[pl.ds(BF16_ROWS, rows), :] = xbc_ref[...]
    buf_ref[pl.ds(BF16_ROWS + rows, BF16_ROWS), :] = zeros
    _conv_silu(buf_ref, n_chunks, smat_ref, cw_ref, cb_ref, p_ref, act_ref)

    a_row = _neg_a_log2(alog_ref)
    lower, upper = _tri()
    dts = [dt_ref[pl.ds(c * CHUNK, CHUNK), :] for c in range(n_chunks)]
    adts = [d * a_row for d in dts]
    wf_exp, totf_exp = _state_weights(_cumsum_rows(adts, lower.astype(BF16)), dts, ef_ref, CHUNK - 1)
    wb_exp, totb_exp = _state_weights(_cumsum_rows(adts, upper.astype(BF16)), dts, eb_ref, 0)
    sf_ref[...] = jnp.zeros(sf_ref.shape, F32)
    sb_ref[...] = jnp.zeros(sb_ref.shape, F32)
    for c in range(n_chunks):
        _chunk_state_update(sf_ref, act_ref, c, wf_exp, totf_exp)
    for c in reversed(range(n_chunks)):
        _chunk_state_update(sb_ref, act_ref, c, wb_exp, totb_exp)


def _conv_scratch(rows):
    return [pltpu.VMEM((rows + 2 * BF16_ROWS, CONV_CH), BF16),
            pltpu.VMEM((2, CONV_K * CONV_WIN, CONV_CH), BF16)]


def _ctx_states(xbc_c, dt_c, smat, conv_w8, conv_b, alog, ef, eb, batch, ctx_len):
    n_chunks = ctx_len // CHUNK
    state = jax.ShapeDtypeStruct((batch, N_SSM_GROUPS, D_STATE, GROUP_W), F32)
    st_spec = pl.BlockSpec((None, N_SSM_GROUPS, D_STATE, GROUP_W), lambda b: (b, 0, 0, 0))
    return pl.pallas_call(
        functools.partial(_ctx_state_kernel, n_chunks),
        grid=(batch,),
        in_specs=[pl.BlockSpec((ctx_len, CONV_CH), lambda b: (b, 0)),
                  pl.BlockSpec((ctx_len, DT_W), lambda b: (b, 0)),
                  _const_spec(smat.shape), _const_spec((SUBLANES, CONV_CH)), _const_spec((1, CONV_CH)),
                  _const_spec((1, LANES)), _const_spec(ef.shape), _const_spec(eb.shape)],
        out_specs=[st_spec, st_spec],
        out_shape=[state, state],
        scratch_shapes=_conv_scratch(ctx_len) + [pltpu.VMEM((ctx_len, CONV_CH), BF16)],
        compiler_params=_cparams(("parallel",)),
        name="ssd_context_states",
    )(xbc_c, dt_c, smat, conv_w8, conv_b, alog, ef, eb)


def _bwd_sweep_kernel(n_blk, cur_ref, prev_ref, next_ref, dt_ref, smat_ref, cw_ref, cb_ref, alog_ref, eb_ref,
                      sb0_ref, act_ref, sin_ref, buf_ref, p_ref, state_ref):
    j = pl.program_id(1)
    blk = n_blk - 1 - j
    rows = SSD_ROWS
    n_chunks = rows // CHUNK

    @pl.when(j == 0)
    def _():
        state_ref[...] = sb0_ref[...]

    zeros = jnp.zeros((BF16_ROWS, CONV_CH), BF16)
    buf_ref[0:BF16_ROWS, :] = jnp.where(blk > 0, prev_ref[...], zeros)
    buf_ref[pl.ds(BF16_ROWS, rows), :] = cur_ref[...]
    buf_ref[pl.ds(BF16_ROWS + rows, BF16_ROWS), :] = jnp.where(blk < n_blk - 1, next_ref[...], zeros)
    _conv_silu(buf_ref, n_chunks, smat_ref, cw_ref, cb_ref, p_ref, act_ref)

    a_row = _neg_a_log2(alog_ref)
    _, upper = _tri()
    dts = [dt_ref[pl.ds(c * CHUNK, CHUNK), :] for c in range(n_chunks)]
    acs_b = _cumsum_rows([d * a_row for d in dts], upper.astype(BF16))
    w_exp, tot_exp = _state_weights(acs_b, dts, eb_ref, 0)
    for c in reversed(range(n_chunks)):
        sin_ref[c] = state_ref[...].astype(BF16)
        _chunk_state_update(state_ref, act_ref, c, w_exp, tot_exp)


def _bwd_sweep(xbc, dt, smat, conv_w8, conv_b, alog, eb, sb0, batch, seq_len):
    t = xbc.shape[0]
    rows = SSD_ROWS
    n_blk = seq_len // rows
    cpb = rows // CHUNK
    halo_per_blk = rows // BF16_ROWS
    n_halo = t // BF16_ROWS

    def blk_of(b, j):
        return b * n_blk + (n_blk - 1 - j)

    return pl.pallas_call(
        functools.partial(_bwd_sweep_kernel, n_blk),
        grid=(batch, n_blk),
        in_specs=[pl.BlockSpec((rows, CONV_CH), lambda b, j: (blk_of(b, j), 0)),
                  pl.BlockSpec((BF16_ROWS, CONV_CH),
                               lambda b, j: (jnp.maximum(blk_of(b, j) * halo_per_blk - 1, 0), 0)),
                  pl.BlockSpec((BF16_ROWS, CONV_CH),
                               lambda b, j: (jnp.minimum((blk_of(b, j) + 1) * halo_per_blk, n_halo - 1), 0)),
                  pl.BlockSpec((rows, DT_W), lambda b, j: (blk_of(b, j), 0)),
                  _const_spec(smat.shape), _const_spec((SUBLANES, CONV_CH)), _const_spec((1, CONV_CH)),
                  _const_spec((1, LANES)), _const_spec(eb.shape),
                  pl.BlockSpec((None, N_SSM_GROUPS, D_STATE, GROUP_W), lambda b, j: (b, 0, 0, 0))],
        out_specs=[pl.BlockSpec((rows, CONV_CH), lambda b, j: (blk_of(b, j), 0)),
                   pl.BlockSpec((cpb, N_SSM_GROUPS, D_STATE, GROUP_W), lambda b, j: (blk_of(b, j), 0, 0, 0))],
        out_shape=[jax.ShapeDtypeStruct((t, CONV_CH), BF16),
                   jax.ShapeDtypeStruct((t // CHUNK, N_SSM_GROUPS, D_STATE, GROUP_W), BF16)],
        scratch_shapes=_conv_scratch(rows) + [pltpu.VMEM((N_SSM_GROUPS, D_STATE, GROUP_W), F32)],
        compiler_params=_cparams(("parallel", "arbitrary")),
        name="ssd_backward_sweep",
    )(xbc, xbc, xbc, dt, smat, conv_w8, conv_b, alog, eb, sb0)


def _fwd_sweep_kernel(act_ref, dt_ref, z_ref, sbin_ref, sf0_ref, alog_ref, ef2_ref, dskip_ref, ng_ref,
                      y_ref, state_ref, ybuf_ref):
    j = pl.program_id(1)

    @pl.when(j == 0)
    def _():
        state_ref[...] = sf0_ref[...]

    a_row = _neg_a_log2(alog_ref)
    lower, upper = _tri()
    lower_bf16 = lower.astype(BF16)
    upper_bf16 = upper.astype(BF16)
    diag = lower & upper
    lane = lax.broadcasted_iota(jnp.int32, (1, LANES), 1)
    lane_lo = lane < SSM_HEAD_DIM
    heads_per_group = N_SSM_HEADS // N_SSM_GROUPS

    n_chunks = SSD_ROWS // CHUNK
    dts = [dt_ref[pl.ds(c * CHUNK, CHUNK), :] for c in range(n_chunks)]
    adts = [d * a_row for d in dts]
    acs_f_all = _cumsum_rows(adts, lower_bf16)
    acs_b_all = _cumsum_rows(adts, upper_bf16)
    adt_ts = [a.T for a in adts]
    acst_f_all = _cumsum_cols(adt_ts, upper_bf16)
    acst_b_all = _cumsum_cols(adt_ts, lower_bf16)
    w_all, tot_all = _state_weights(acs_f_all, dts, ef2_ref, CHUNK - 1)

    for c in range(n_chunks):
        rows = pl.ds(c * CHUNK, CHUNK)
        crows = slice(c * CHUNK, (c + 1) * CHUNK)
        acs_f, acs_b = acs_f_all[c], acs_b_all[c]
        dt_t = dts[c].T
        log2_dt_t = jnp.log2(dt_t)
        row_f = acst_f_all[c] - log2_dt_t
        row_b = acst_b_all[c] - log2_dt_t
        diag_log = jnp.log2(dt_t[0:N_SSM_HEADS] + dt_t[N_SSM_HEADS:2 * N_SSM_HEADS])

        b_chunk = act_ref[rows, D_INNER:D_INNER + BC_W]
        c_chunk = act_ref[rows, D_INNER + BC_W:CONV_CH]

        for g in range(N_SSM_GROUPS):
            c_g = c_chunk[:, g * D_STATE:(g + 1) * D_STATE]
            cb = _dot_nt(c_g, b_chunk[:, g * D_STATE:(g + 1) * D_STATE])
            c_g32 = c_g.astype(F32)
            for pair in range(g * heads_per_group // 2, (g + 1) * heads_per_group // 2):
                pcols = slice((pair % (heads_per_group // 2)) * LANES, (pair % (heads_per_group // 2) + 1) * LANES)
                xs_pair = act_ref[rows, pair * LANES:(pair + 1) * LANES]
                sf_pair = state_ref[g, :, pcols].astype(BF16)
                sb_pair = sbin_ref[c, g, :, pcols]
                outs = []
                for h in (2 * pair, 2 * pair + 1):
                    hb = N_SSM_HEADS + h
                    col_f = jnp.broadcast_to(acs_f[:, h:h + 1], (CHUNK, CHUNK))
                    col_b = jnp.broadcast_to(acs_b[:, hb:hb + 1], (CHUNK, CHUNK))
                    seg = jnp.where(diag, diag_log[h:h + 1, :],
                                    jnp.where(lower, col_f - row_f[h:h + 1, :], col_b - row_b[hb:hb + 1, :]))
                    outs.append(_dot((cb * jnp.exp2(seg)).astype(BF16), xs_pair)
                                + _dot((c_g32 * jnp.exp2(col_f)).astype(BF16), sf_pair)
                                + _dot((c_g32 * jnp.exp2(col_b)).astype(BF16), sb_pair))
                ybuf_ref[:, pair * LANES:(pair + 1) * LANES] = jnp.where(lane_lo, outs[0], outs[1])

        for g in range(N_SSM_GROUPS):
            cols = slice(g * GROUP_W, (g + 1) * GROUP_W)
            y = ybuf_ref[:, cols] + dskip_ref[:, cols] * act_ref[rows, cols].astype(F32)
            y = y * z_ref[rows, cols].astype(F32)
            ms = jnp.mean(y * y, axis=-1, keepdims=True)
            y_ref[rows, cols] = (y * lax.rsqrt(ms + EPS) * ng_ref[:, cols]).astype(BF16)

        _state_update(state_ref, b_chunk, act_ref[rows, 0:D_INNER], w_all[crows], tot_all[c:c + 1])


def _fwd_sweep(act, dt, z, sb_in, sf0, alog, ef2, dskip, norm_g, batch, seq_len):
    t = act.shape[0]
    rows = SSD_ROWS
    n_blk = seq_len // rows
    cpb = rows // CHUNK
    blk = lambda b, j: b * n_blk + j
    return pl.pallas_call(
        _fwd_sweep_kernel,
        grid=(batch, n_blk),
        in_specs=[pl.BlockSpec((rows, CONV_CH), lambda b, j: (blk(b, j), 0)),
                  pl.BlockSpec((rows, DT_W), lambda b, j: (blk(b, j), 0)),
                  pl.BlockSpec((rows, D_INNER), lambda b, j: (blk(b, j), 0)),
                  pl.BlockSpec((cpb, N_SSM_GROUPS, D_STATE, GROUP_W), lambda b, j: (blk(b, j), 0, 0, 0)),
                  pl.BlockSpec((None, N_SSM_GROUPS, D_STATE, GROUP_W), lambda b, j: (b, 0, 0, 0)),
                  _const_spec((1, LANES)), _const_spec(ef2.shape),
                  _const_spec((1, D_INNER)), _const_spec((1, D_INNER))],
        out_specs=pl.BlockSpec((rows, D_INNER), lambda b, j: (blk(b, j), 0)),
        out_shape=jax.ShapeDtypeStruct((t, D_INNER), BF16),
        scratch_shapes=[pltpu.VMEM((N_SSM_GROUPS, D_STATE, GROUP_W), F32),
                        pltpu.VMEM((CHUNK, D_INNER), F32)],
        compiler_params=_cparams(("parallel", "arbitrary")),
        name="ssd_forward_sweep",
    )(act, dt, z, sb_in, sf0, alog, ef2, dskip, norm_g)


FF_SPLIT = 1280


def _merge_ffn_kernel(x_ref, ya_ref, ys_ref, ga_ref, gb_ref, g1_ref, sc_ref, sh_ref, g2_ref, ng_ref, fg_ref,
                      woa_ref, wob_ref, wout_ref, win_ref, wffo_ref, o_ref):
    half_rows = ROW_TILE // 2
    halves = [slice(k * half_rows, (k + 1) * half_rows) for k in range(2)]

    def merge(r):
        a = _dot(ya_ref[r, :], woa_ref[...])
        b = _dot(ys_ref[r, :], wob_ref[...])
        merged = (ga_ref[r, :].astype(F32) * a + gb_ref[r, :].astype(F32) * b).astype(BF16)
        return x_ref[r, :] + g1_ref[...] * _dot(merged, wout_ref[...])

    def swiglu(h):
        acc = None
        for c0, c1 in ((0, FF_SPLIT), (FF_SPLIT, D_FF)):
            gate = _dot(h, win_ref[:, c0:c1])
            up = _dot(h, win_ref[:, D_FF + c0:D_FF + c1])
            part = _dot((_silu(gate) * up).astype(BF16), wffo_ref[c0:c1, :])
            acc = part if acc is None else acc + part
        return acc

    x1 = [merge(r) for r in halves]
    hs = [_adaln(v, ng_ref, sc_ref, sh_ref).astype(BF16) for v in x1]
    ff = [swiglu(h) for h in hs]
    for r, v, f in zip(halves, x1, ff):
        x2 = v + g2_ref[...] * f
        ms = jnp.mean(x2 * x2, axis=-1, keepdims=True)
        o_ref[r, :] = x2 * lax.rsqrt(ms + EPS) * fg_ref[...]


def _merge_ffn(x2, y_att, y_ssm, ga, gb, g1, sc2, sh2, g2, norm_g, final_g, w_oa, w_ob, w_out, w_ffn_in, w_ffn_out,
               seq_len):
    t, d = x2.shape
    tm = ROW_TILE
    per_seq = seq_len // tm
    row = lambda w: pl.BlockSpec((tm, w), lambda i: (i, 0))
    mod_spec = pl.BlockSpec((None, 1, d), lambda i: (i // per_seq, 0, 0))
    return pl.pallas_call(
        _merge_ffn_kernel,
        grid=(t // tm,),
        in_specs=[row(d), row(ATT_W), row(D_INNER), row(d), row(d), mod_spec, mod_spec, mod_spec, mod_spec,
                  _const_spec((1, d)), _const_spec((1, d)),
                  _const_spec(w_oa.shape), _const_spec(w_ob.shape), _const_spec(w_out.shape),
                  _const_spec(w_ffn_in.shape), _const_spec(w_ffn_out.shape)],
        out_specs=row(d),
        out_shape=jax.ShapeDtypeStruct((t, d), F32),
        compiler_params=_cparams(("parallel",)),
        name="merge_ffn_final_norm",
    )(x2, y_att, y_ssm, ga, gb, g1, sc2, sh2, g2, norm_g, final_g, w_oa, w_ob, w_out, w_ffn_in, w_ffn_out)


def _rope_tables(seq_len):
    rows = seq_len // GRID_W
    row = jnp.repeat(jnp.arange(rows), GRID_W).astype(F32)
    col = jnp.tile(jnp.arange(GRID_W), rows).astype(F32)
    inv = ROPE_BASE ** (-jnp.arange(ROPE_FREQS, dtype=F32) / ROPE_FREQS)
    ang_r = row[:, None] * inv
    ang_c = col[:, None] * inv
    ang = jnp.concatenate([ang_r, ang_r, ang_c, ang_c] * 2, axis=-1)
    sign = np.where((np.arange(LANES) % (2 * ROPE_FREQS)) < ROPE_FREQS, -1.0, 1.0).astype(np.float32)
    return jnp.cos(ang), jnp.sin(ang) * jnp.asarray(sign)[None, :]


def _expand_matrix(first_row):
    e = np.zeros((LANES, D_INNER), np.float32)
    for h in range(N_SSM_HEADS):
        e[first_row + h, h * SSM_HEAD_DIM:(h + 1) * SSM_HEAD_DIM] = 1.0
    return jnp.asarray(e, dtype=BF16)


def _conv_shift_matrix():
    s = np.zeros((CHUNK, CONV_K * CONV_WIN), np.float32)
    for k in range(CONV_K):
        off = k + BF16_ROWS - CONV_K // 2 if k <= CONV_K // 2 else k - CONV_K // 2
        for t in range(CHUNK):
            s[t, k * CONV_WIN + t + off] = 1.0
    return jnp.asarray(s, dtype=BF16)


def kernel(x, c, ctx, c_ctx, w_mod, b_mod, norm1_g, w_in, attn_sink, conv_w, conv_b, a_log_f, a_log_b,
           dt_bias_f, dt_bias_b, d_skip, ssm_norm_g, w_oa, w_ob, w_out, norm2_g, w_ffn_in, w_ffn_out, final_g):
    batch, seq_len, d = x.shape
    ctx_len = ctx.shape[1]
    assert w_mod.shape[0] == 1 and d == D_MODEL
    assert seq_len % SSD_ROWS == 0 and seq_len % ROW_TILE == 0 and seq_len % GRID_W == 0
    assert ctx_len % CHUNK == 0

    pad_rows = (-(batch + 1)) % SUBLANES
    c_all = jnp.concatenate([c, c_ctx[None, :], jnp.zeros((pad_rows, d), F32)], axis=0)
    layer = lambda p: p.reshape(p.shape[1:])
    mod = _modulation(c_all, layer(w_mod), b_mod[0][None, :])
    mod_b = mod[:batch].reshape(batch, N_MOD, 1, d)
    sh1, sc1, g1, sh2, sc2, g2 = (mod_b[:, k] for k in range(N_MOD))
    mod_c = mod[batch:batch + 1].reshape(1, N_MOD, 1, d)
    sh1c, sc1c = mod_c[:, 0], mod_c[:, 1]

    w = layer(w_in).astype(BF16)
    o = np.cumsum([0, ATT_W, KV_W, KV_W, D_INNER, CONV_CH, N_SSM_HEADS, N_SSM_HEADS, D_MODEL, D_MODEL])
    seg = lambda k: w[:, o[k]:o[k + 1]]
    w_main = jnp.concatenate(
        [seg(0), seg(1), seg(2), seg(3), seg(4), seg(7), seg(8), seg(5), seg(6),
         jnp.zeros((d, DT_W - 2 * N_SSM_HEADS), BF16)], axis=1)
    pad_heads = jnp.zeros((DT_W - 2 * N_SSM_HEADS,), F32)
    dt_bias = jnp.concatenate([dt_bias_f[0], dt_bias_b[0], pad_heads])[None, :]
    alog = jnp.concatenate([a_log_f[0], a_log_b[0], pad_heads])[None, :]
    conv_w8 = jnp.concatenate([conv_w[0], jnp.zeros((SUBLANES - CONV_K, CONV_CH), F32)], axis=0)
    conv_b1 = conv_b[0][None, :]
    cos, sin = _rope_tables(seq_len)
    ef2 = _expand_matrix(0)
    eb2 = _expand_matrix(N_SSM_HEADS)
    g_norm1 = norm1_g[0][None, :]

    x2 = x.reshape(batch * seq_len, d)
    xc2 = ctx.reshape(batch * ctx_len, d)

    kv_c, xbc_c, dt_c = _inproj_context(xc2, sc1c, sh1c, g_norm1, w_main, dt_bias, ctx_len)
    z, xbc, dt, ga, gb, y_att = _inproj_attention(x2, sc1, sh1, g_norm1, w_main, dt_bias, cos, sin, kv_c,
                                                  attn_sink[0], seq_len, ctx_len)

    smat = _conv_shift_matrix()
    sf0, sb0 = _ctx_states(xbc_c, dt_c, smat, conv_w8, conv_b1, alog, ef2, eb2, batch, ctx_len)
    act, sb_in = _bwd_sweep(xbc, dt, smat, conv_w8, conv_b1, alog, eb2, sb0, batch, seq_len)
    dskip = jnp.repeat(d_skip[0], SSM_HEAD_DIM)[None, :]
    y_ssm = _fwd_sweep(act, dt, z, sb_in, sf0, alog, ef2, dskip, ssm_norm_g[0][None, :], batch, seq_len)

    out = _merge_ffn(x2, y_att, y_ssm, ga, gb, g1, sc2, sh2, g2, norm2_g[0][None, :], final_g[None, :],
                     layer(w_oa).astype(BF16), layer(w_ob).astype(BF16), layer(w_out).astype(BF16),
                     layer(w_ffn_in).astype(BF16), layer(w_ffn_out).astype(BF16), seq_len)
    return out.reshape(batch, seq_len, d)
```

```python
import functools

import numpy as np
import jax
import jax.numpy as jnp
from jax import lax
from jax.experimental import pallas as pl
from jax.experimental.pallas import tpu as pltpu

F32 = jnp.float32
BF16 = jnp.bfloat16

D_MODEL = 1024
GRID_W = 64
EPS = 1e-6
HEAD_DIM = 64
N_Q_HEADS = 16
N_KV_HEADS = 4
ATT_W = N_Q_HEADS * HEAD_DIM
KV_W = N_KV_HEADS * HEAD_DIM
WINDOW = 128
ROPE_BASE = 10000.0
ROPE_FREQS = HEAD_DIM // 4
D_INNER = 2 * D_MODEL
SSM_HEAD_DIM = 64
N_SSM_HEADS = D_INNER // SSM_HEAD_DIM
N_SSM_GROUPS = 4
GROUP_W = D_INNER // N_SSM_GROUPS
D_STATE = 128
BC_W = N_SSM_GROUPS * D_STATE
CONV_K = 5
CONV_CH = D_INNER + 2 * BC_W
CHUNK = 128
CONV_WIN = CHUNK + 16
D_FF = -(-8 * D_MODEL // (3 * 256)) * 256
N_MOD = 6

LANES = 128
SUBLANES = 8
BF16_ROWS = 16
VMEM_LIMIT = 56 * 1024 * 1024

OFF_Q = 0
OFF_K = OFF_Q + ATT_W
OFF_V = OFF_K + KV_W
OFF_Z = OFF_V + KV_W
OFF_XBC = OFF_Z + D_INNER
OFF_GA = OFF_XBC + CONV_CH
OFF_GB = OFF_GA + D_MODEL
OFF_DT = OFF_GB + D_MODEL
DT_W = LANES
N_PROJ = OFF_DT + DT_W
KV_CAT = 2 * N_KV_HEADS * LANES
V_OFF = N_KV_HEADS * LANES
LOG2E = 1.4426950408889634

ROW_TILE = 512
SSD_ROWS = 512
ATT_BLOCK = 128


def _cparams(sem):
    return pltpu.CompilerParams(dimension_semantics=sem, vmem_limit_bytes=VMEM_LIMIT)


def _const_spec(shape):
    n = len(shape)
    return pl.BlockSpec(shape, lambda *_: (0,) * n, pipeline_mode=pl.Buffered(1))


def _dot(a, b):
    return jnp.dot(a, b, preferred_element_type=F32)


def _dot_nt(a, b):
    return lax.dot_general(a, b, (((1,), (1,)), ((), ())), preferred_element_type=F32)


def _dot_tn(a, b):
    return lax.dot_general(a, b, (((0,), (0,)), ((), ())), preferred_element_type=F32)


def _sigmoid(v):
    return 1.0 / (1.0 + jnp.exp(-v))


def _silu(v):
    return v * _sigmoid(v)


def _split3(v):
    hi = v.astype(BF16)
    r = v - hi.astype(F32)
    mid = r.astype(BF16)
    lo = (r - mid.astype(F32)).astype(BF16)
    return hi, mid, lo


def _mod_kernel(c_ref, w_ref, b_ref, o_ref):
    act = _silu(c_ref[...])
    o_ref[...] = jnp.dot(act, w_ref[...], preferred_element_type=F32,
                         precision=lax.Precision.HIGHEST) + b_ref[...]


def _modulation(c_all, w_mod, b_mod):
    rows, d = c_all.shape
    n = w_mod.shape[1]
    tn = 1024
    return pl.pallas_call(
        _mod_kernel,
        grid=(n // tn,),
        in_specs=[pl.BlockSpec((rows, d), lambda j: (0, 0)),
                  pl.BlockSpec((d, tn), lambda j: (0, j)),
                  pl.BlockSpec((1, tn), lambda j: (0, j))],
        out_specs=pl.BlockSpec((rows, tn), lambda j: (0, j)),
        out_shape=jax.ShapeDtypeStruct((rows, n), F32),
        compiler_params=_cparams(("arbitrary",)),
        name="modulation",
    )(c_all, w_mod, b_mod)


def _adaln(x, g_ref, sc_ref, sh_ref):
    ms = jnp.mean(x * x, axis=-1, keepdims=True)
    xn = x * lax.rsqrt(ms + EPS)
    return xn * (g_ref[...] * (1.0 + sc_ref[...])) + sh_ref[...]


def _store_padded_heads(val, col0, kv_ref):
    lane = lax.broadcasted_iota(jnp.int32, (1, LANES), 1)
    is_lo = lane < HEAD_DIM
    for s in range(2):
        slab = val[:, s * LANES:(s + 1) * LANES]
        swapped = pltpu.roll(slab, HEAD_DIM, 1)
        for h, src in ((2 * s, slab), (2 * s + 1, swapped)):
            kv_ref[:, col0 + h * LANES: col0 + (h + 1) * LANES] = jnp.where(is_lo, src, 0.0).astype(BF16)


def _rope_slab(slab, cos, sin_signed, first_half):
    fwd = pltpu.roll(slab, LANES - ROPE_FREQS, 1)
    bwd = pltpu.roll(slab, ROPE_FREQS, 1)
    return slab * cos + jnp.where(first_half, fwd, bwd) * sin_signed


PROJ_PIECE = 1024


def _softplus(v):
    return jnp.maximum(v, 0.0) + jnp.log1p(jnp.exp(-jnp.abs(v)))


def _project_kv(proj, rope, kv_ref):
    k = proj(OFF_K, KV_W)
    if rope is not None:
        k = jnp.concatenate([_rope_slab(k[:, s * LANES:(s + 1) * LANES], *rope) for s in range(2)], axis=1)
    _store_padded_heads(k, 0, kv_ref)
    _store_padded_heads(proj(OFF_V, KV_W), V_OFF, kv_ref)


def _inproj_ctx_kernel(x_ref, sc_ref, sh_ref, g_ref, w_ref, dtb_ref, kv_ref, xbc_ref, dt_ref):
    h = _adaln(x_ref[...], g_ref, sc_ref, sh_ref).astype(BF16)
    proj = lambda off, width: _dot(h, w_ref[:, off:off + width])
    _project_kv(proj, None, kv_ref)
    for c0 in range(0, CONV_CH, PROJ_PIECE):
        xbc_ref[:, c0:c0 + PROJ_PIECE] = proj(OFF_XBC + c0, PROJ_PIECE).astype(BF16)
    dt_ref[...] = _softplus(proj(OFF_DT, DT_W) + dtb_ref[...])


def _inproj_context(xc2, sc, sh, g, w_main, dt_bias, ctx_len):
    t, d = xc2.shape
    tm = ctx_len
    mod_spec = pl.BlockSpec((None, 1, d), lambda i: (0, 0, 0))
    row = lambda w: pl.BlockSpec((tm, w), lambda i: (i, 0))
    sds = lambda w, dt: jax.ShapeDtypeStruct((t, w), dt)
    return pl.pallas_call(
        _inproj_ctx_kernel,
        grid=(t // tm,),
        in_specs=[row(d), mod_spec, mod_spec, _const_spec((1, d)), _const_spec((d, N_PROJ)),
                  _const_spec((1, DT_W))],
        out_specs=[row(KV_CAT), row(CONV_CH), row(DT_W)],
        out_shape=[sds(KV_CAT, BF16), sds(CONV_CH, BF16), sds(DT_W, F32)],
        compiler_params=_cparams(("parallel",)),
        name="inproj_context",
    )(xc2, sc, sh, g, w_main, dt_bias)


def _attn_scores(q_slabs, k_parts):
    half = HEAD_DIM

    def swap_halves(t):
        return jnp.concatenate([t[half:], t[:half]], axis=0)

    t0, t1 = (q.astype(F32).T.astype(BF16) for q in q_slabs)
    q_t = jnp.concatenate([t0, swap_halves(t0), t1, swap_halves(t1)], axis=1)
    return _dot(jnp.concatenate(k_parts, axis=0), q_t)


def _attn_finish(s, v_parts, bias_prev, bias_next, sink_row):
    half = HEAD_DIM
    s_p = s[0:ATT_BLOCK] + bias_prev
    s_c = s[ATT_BLOCK:2 * ATT_BLOCK]
    s_n = s[2 * ATT_BLOCK:3 * ATT_BLOCK] + bias_next
    s_x = s[3 * ATT_BLOCK:]
    red = lambda f, parts: functools.reduce(f, parts)
    m = red(jnp.maximum, [jnp.max(p, axis=0, keepdims=True) for p in (s_p, s_c, s_n, s_x)])
    m = jnp.maximum(m, sink_row)
    e = [jnp.exp2(p - m) for p in (s_p, s_c, s_n, s_x)]
    denom = red(jnp.add, [jnp.sum(p, axis=0, keepdims=True) for p in e]) + jnp.exp2(sink_row - m)
    p_all = jnp.concatenate([p.astype(BF16) for p in e], axis=0)
    v_t = jnp.concatenate(v_parts, axis=0).T[:half]
    out_t = _dot(v_t, p_all) * (1.0 / denom)
    return [jnp.concatenate([out_t[:, (2 * sl) * ATT_BLOCK:(2 * sl + 1) * ATT_BLOCK],
                             out_t[:, (2 * sl + 1) * ATT_BLOCK:(2 * sl + 2) * ATT_BLOCK]], axis=0).T
            for sl in range(2)]


def _inproj_attn_kernel(per_seq, sink_ref, x_ref, sc_ref, sh_ref, g_ref, w_ref, dtb_ref, cos_ref, sin_ref, kvx_ref,
                        z_ref, xbc_ref, dt_ref, ga_ref, gb_ref, ya_ref,
                        q_cur, q_nxt, kv_cur, kv_nxt, kv_tail):
    i = pl.program_id(0)
    blocks = ROW_TILE // ATT_BLOCK
    group = N_Q_HEADS // N_KV_HEADS
    width = group * ATT_BLOCK

    @pl.when(i == 0)
    def _():
        q_cur[...] = jnp.zeros(q_cur.shape, BF16)
        kv_cur[...] = jnp.zeros(kv_cur.shape, BF16)
        kv_tail[...] = jnp.zeros(kv_tail.shape, BF16)

    def parts(sub, kvh, col0):
        qrows = slice(sub * ATT_BLOCK, (sub + 1) * ATT_BLOCK)
        cols = slice(col0 + kvh * LANES, col0 + (kvh + 1) * LANES)
        prev = kv_tail[:, cols] if sub == 0 else kv_cur[(sub - 1) * ATT_BLOCK:sub * ATT_BLOCK, cols]
        nxt = (kv_nxt[0:ATT_BLOCK, cols] if sub == blocks - 1
               else kv_cur[(sub + 1) * ATT_BLOCK:(sub + 2) * ATT_BLOCK, cols])
        return [prev, kv_cur[qrows, cols], nxt, kvx_ref[:, cols]]

    def scores(sub, kvh):
        qrows = slice(sub * ATT_BLOCK, (sub + 1) * ATT_BLOCK)
        q_slabs = [q_cur[qrows, (2 * kvh + sl) * LANES:(2 * kvh + sl + 1) * LANES] for sl in range(2)]
        return _attn_scores(q_slabs, parts(sub, kvh, 0))

    work = [(sub, kvh) for sub in range(blocks) for kvh in range(N_KV_HEADS)]
    s_next = scores(*work[0])

    h = _adaln(x_ref[...], g_ref, sc_ref, sh_ref).astype(BF16)
    proj = lambda off, wid: _dot(h, w_ref[:, off:off + wid])
    cos = cos_ref[...]
    sin = sin_ref[...]
    lane = lax.broadcasted_iota(jnp.int32, (1, LANES), 1)
    first_half = (lane % (2 * ROPE_FREQS)) < ROPE_FREQS
    _project_kv(proj, (cos, sin, first_half), kv_nxt)
    scale = HEAD_DIM ** -0.5 * LOG2E
    q = proj(OFF_Q, ATT_W)
    for s in range(ATT_W // LANES):
        q_nxt[:, s * LANES:(s + 1) * LANES] = (
            _rope_slab(q[:, s * LANES:(s + 1) * LANES], cos, sin, first_half) * scale).astype(BF16)

    piece = PROJ_PIECE // 2

    def store_xbc(c0):
        xbc_ref[:, c0:c0 + piece] = proj(OFF_XBC + c0, piece).astype(BF16)

    def store_z(c0):
        z_ref[:, c0:c0 + piece] = _silu(proj(OFF_Z + c0, piece)).astype(BF16)

    def store_dt():
        dt_ref[...] = _softplus(proj(OFF_DT, DT_W) + dtb_ref[...])

    def store_gate(ref, off, c0):
        ref[:, c0:c0 + piece] = _sigmoid(proj(off + c0, piece)).astype(BF16)

    plain = [functools.partial(store_xbc, c0) for c0 in range(0, CONV_CH, piece)] + [store_dt]
    gated = ([functools.partial(store_z, c0) for c0 in range(0, D_INNER, piece)]
             + [functools.partial(store_gate, ga_ref, OFF_GA, c0) for c0 in range(0, D_MODEL, piece)]
             + [functools.partial(store_gate, gb_ref, OFF_GB, c0) for c0 in range(0, D_MODEL, piece)])
    segments = [s for pair in zip(plain, gated) for s in pair] + gated[len(plain):] + plain[len(gated):]

    seq_tile = (i + per_seq - 1) % per_seq
    key_j = lax.broadcasted_iota(jnp.int32, (ATT_BLOCK, width), 0)
    lane_w = lax.broadcasted_iota(jnp.int32, (ATT_BLOCK, width), 1)
    query_r = lane_w % ATT_BLOCK
    neg = jnp.float32(-1e30)
    band_prev = key_j >= query_r
    band_next = key_j <= query_r
    lane1 = lax.broadcasted_iota(jnp.int32, (1, width), 1)

    for n, (sub, kvh) in enumerate(work):
        s_cur = s_next
        if n + 1 < len(work):
            s_next = scores(*work[n + 1])
        if segments:
            segments.pop(0)()
        qrows = slice(sub * ATT_BLOCK, (sub + 1) * ATT_BLOCK)
        prev_ok = (seq_tile > 0) if sub == 0 else True
        next_ok = (seq_tile < per_seq - 1) if sub == blocks - 1 else True
        bias_prev = jnp.where(band_prev & prev_ok, 0.0, neg)
        bias_next = jnp.where(band_next & next_ok, 0.0, neg)
        sink_row = jnp.zeros((1, width), F32)
        for g in range(group):
            sink_row = jnp.where(lane1 // ATT_BLOCK == g, sink_ref[kvh * group + g] * LOG2E, sink_row)
        outs = _attn_finish(s_cur, parts(sub, kvh, V_OFF), bias_prev, bias_next, sink_row)
        for sl in range(2):
            ya_ref[qrows, (2 * kvh + sl) * LANES:(2 * kvh + sl + 1) * LANES] = outs[sl].astype(BF16)
    for seg in segments:
        seg()

    kv_tail[...] = kv_cur[ROW_TILE - ATT_BLOCK:, :]
    kv_cur[...] = kv_nxt[...]
    q_cur[...] = q_nxt[...]


def _inproj_attention(x2, sc, sh, g, w_main, dt_bias, cos, sin, kv_ctx, sink, seq_len, ctx_len):
    t, d = x2.shape
    tm = ROW_TILE
    per_seq = seq_len // tm
    n_tiles = t // tm
    cur = lambda i: jnp.minimum(i, n_tiles - 1)
    prev = lambda i: jnp.maximum(i - 1, 0)
    mod_spec = pl.BlockSpec((None, 1, d), lambda i: (cur(i) // per_seq, 0, 0))
    pos_spec = pl.BlockSpec((tm, LANES), lambda i: (cur(i) % per_seq, 0))
    row = lambda w: pl.BlockSpec((tm, w), lambda i: (cur(i), 0))
    sds = lambda w, dt: jax.ShapeDtypeStruct((t, w), dt)
    return pl.pallas_call(
        functools.partial(_inproj_attn_kernel, per_seq),
        grid=(n_tiles + 1,),
        in_specs=[pl.BlockSpec(memory_space=pltpu.SMEM), row(d), mod_spec, mod_spec, _const_spec((1, d)),
                  _const_spec((d, N_PROJ)), _const_spec((1, DT_W)), pos_spec, pos_spec,
                  pl.BlockSpec((ctx_len, KV_CAT), lambda i: (prev(i) // per_seq, 0))],
        out_specs=[row(D_INNER), row(CONV_CH), row(DT_W), row(d), row(d),
                   pl.BlockSpec((tm, ATT_W), lambda i: (prev(i), 0))],
        out_shape=[sds(D_INNER, BF16), sds(CONV_CH, BF16), sds(DT_W, F32), sds(d, BF16), sds(d, BF16),
                   sds(ATT_W, BF16)],
        scratch_shapes=[pltpu.VMEM((tm, ATT_W), BF16), pltpu.VMEM((tm, ATT_W), BF16),
                        pltpu.VMEM((tm, KV_CAT), BF16), pltpu.VMEM((tm, KV_CAT), BF16),
                        pltpu.VMEM((ATT_BLOCK, KV_CAT), BF16)],
        compiler_params=_cparams(("arbitrary",)),
        name="inproj_attention",
    )(sink, x2, sc, sh, g, w_main, dt_bias, cos, sin, kv_ctx)


def _conv_silu(buf_ref, n_blocks, smat_ref, w_ref, b_ref, p_ref, out_ref):
    for m in range(n_blocks):
        r0 = m * CHUNK
        slot = m % 2
        for k in range(CONV_K):
            start = r0 if k <= CONV_K // 2 else r0 + BF16_ROWS
            p_ref[slot, k * CONV_WIN:(k + 1) * CONV_WIN, :] = (
                buf_ref[pl.ds(start, CONV_WIN), :] * w_ref[k:k + 1, :].astype(BF16))
        acc = _dot(smat_ref[...], p_ref[slot]) + b_ref[...]
        out_ref[pl.ds(r0, CHUNK), :] = _silu(acc).astype(out_ref.dtype)


def _tri():
    r = lax.broadcasted_iota(jnp.int32, (CHUNK, CHUNK), 0)
    c = lax.broadcasted_iota(jnp.int32, (CHUNK, CHUNK), 1)
    return r >= c, r <= c


def _cumsum_rows(adts, tri_bf16):
    x3 = jnp.concatenate([t for a in adts for t in _split3(a)], axis=1)
    c = _dot(tri_bf16, x3)
    w = 3 * LANES
    return [c[:, i * w:i * w + LANES] + c[:, i * w + LANES:i * w + 2 * LANES] + c[:, i * w + 2 * LANES:(i + 1) * w]
            for i in range(len(adts))]


def _cumsum_cols(adt_ts, tri_bf16):
    x3 = jnp.concatenate([t for a in adt_ts for t in _split3(a)], axis=0)
    c = _dot(x3, tri_bf16)
    h = 3 * CHUNK
    return [c[i * h:i * h + CHUNK] + c[i * h + CHUNK:i * h + 2 * CHUNK] + c[i * h + 2 * CHUNK:(i + 1) * h]
            for i in range(len(adt_ts))]


def _expand(v, e_ref):
    return _dot(v.astype(BF16), e_ref[...])


def _state_update(state_ref, b_chunk, xs_chunk, w_exp, tot_exp):
    for g in range(N_SSM_GROUPS):
        cols = slice(g * GROUP_W, (g + 1) * GROUP_W)
        xw = (xs_chunk[:, cols].astype(F32) * w_exp[:, cols]).astype(BF16)
        upd = _dot_tn(b_chunk[:, g * D_STATE:(g + 1) * D_STATE], xw)
        state_ref[g] = state_ref[g] * tot_exp[:, cols] + upd


def _neg_a_log2(alog_ref):
    return -jnp.exp(alog_ref[...]) * LOG2E


def _state_weights(acs, dts, e_ref, total_row):
    totals = [a[total_row:total_row + 1, :] for a in acs]
    w = jnp.concatenate([jnp.exp2(t - a) * d for t, a, d in zip(totals, acs, dts)], axis=0)
    pad = jnp.zeros((SUBLANES - len(acs), LANES), F32)
    tot = jnp.concatenate([jnp.exp2(t) for t in totals] + [pad], axis=0)
    return _expand(w, e_ref), _expand(tot, e_ref)


def _chunk_state_update(state_ref, act_ref, c, w_exp, tot_exp):
    rows = pl.ds(c * CHUNK, CHUNK)
    _state_update(state_ref, act_ref[rows, D_INNER:D_INNER + BC_W], act_ref[rows, 0:D_INNER],
                  w_exp[c * CHUNK:(c + 1) * CHUNK], tot_exp[c:c + 1])


def _ctx_state_kernel(n_chunks, xbc_ref, dt_ref, smat_ref, cw_ref, cb_ref, alog_ref, ef_ref, eb_ref,
                      sf_ref, sb_ref, buf_ref, p_ref, act_ref):
    rows = n_chunks * CHUNK
    zeros = jnp.zeros((BF16_ROWS, CONV_CH), BF16)
    buf_ref[0:BF16_ROWS, :] = zeros
    buf_ref[pl.ds(BF16_ROWS, rows), :] = xbc_ref[...]
    buf_ref[pl.ds(BF16_ROWS + rows, BF16_ROWS), :] = zeros
    _conv_silu(buf_ref, n_chunks, smat_ref, cw_ref, cb_ref, p_ref, act_ref)

    a_row = _neg_a_log2(alog_ref)
    lower, upper = _tri()
    dts = [dt_ref[pl.ds(c * CHUNK, CHUNK), :] for c in range(n_chunks)]
    adts = [d * a_row for d in dts]
    wf_exp, totf_exp = _state_weights(_cumsum_rows(adts, lower.astype(BF16)), dts, ef_ref, CHUNK - 1)
    wb_exp, totb_exp = _state_weights(_cumsum_rows(adts, upper.astype(BF16)), dts, eb_ref, 0)
    sf_ref[...] = jnp.zeros(sf_ref.shape, F32)
    sb_ref[...] = jnp.zeros(sb_ref.shape, F32)
    for c in range(n_chunks):
        _chunk_state_update(sf_ref, act_ref, c, wf_exp, totf_exp)
    for c in reversed(range(n_chunks)):
        _chunk_state_update(sb_ref, act_ref, c, wb_exp, totb_exp)


def _conv_scratch(rows):
    return [pltpu.VMEM((rows + 2 * BF16_ROWS, CONV_CH), BF16),
            pltpu.VMEM((2, CONV_K * CONV_WIN, CONV_CH), BF16)]


def _ctx_states(xbc_c, dt_c, smat, conv_w8, conv_b, alog, ef, eb, batch, ctx_len):
    n_chunks = ctx_len // CHUNK
    state = jax.ShapeDtypeStruct((batch, N_SSM_GROUPS, D_STATE, GROUP_W), F32)
    st_spec = pl.BlockSpec((None, N_SSM_GROUPS, D_STATE, GROUP_W), lambda b: (b, 0, 0, 0))
    return pl.pallas_call(
        functools.partial(_ctx_state_kernel, n_chunks),
        grid=(batch,),
        in_specs=[pl.BlockSpec((ctx_len, CONV_CH), lambda b: (b, 0)),
                  pl.BlockSpec((ctx_len, DT_W), lambda b: (b, 0)),
                  _const_spec(smat.shape), _const_spec((SUBLANES, CONV_CH)), _const_spec((1, CONV_CH)),
                  _const_spec((1, LANES)), _const_spec(ef.shape), _const_spec(eb.shape)],
        out_specs=[st_spec, st_spec],
        out_shape=[state, state],
        scratch_shapes=_conv_scratch(ctx_len) + [pltpu.VMEM((ctx_len, CONV_CH), BF16)],
        compiler_params=_cparams(("parallel",)),
        name="ssd_context_states",
    )(xbc_c, dt_c, smat, conv_w8, conv_b, alog, ef, eb)


def _bwd_sweep_kernel(n_blk, cur_ref, prev_ref, next_ref, dt_ref, smat_ref, cw_ref, cb_ref, alog_ref, eb_ref,
                      sb0_ref, act_ref, sin_ref, buf_ref, p_ref, state_ref):
    j = pl.program_id(1)
    blk = n_blk - 1 - j
    rows = SSD_ROWS
    n_chunks = rows // CHUNK

    @pl.when(j == 0)
    def _():
        state_ref[...] = sb0_ref[...]

    zeros = jnp.zeros((BF16_ROWS, CONV_CH), BF16)
    buf_ref[0:BF16_ROWS, :] = jnp.where(blk > 0, prev_ref[...], zeros)
    buf_ref[pl.ds(BF16_ROWS, rows), :] = cur_ref[...]
    buf_ref[pl.ds(BF16_ROWS + rows, BF16_ROWS), :] = jnp.where(blk < n_blk - 1, next_ref[...], zeros)
    _conv_silu(buf_ref, n_chunks, smat_ref, cw_ref, cb_ref, p_ref, act_ref)

    a_row = _neg_a_log2(alog_ref)
    _, upper = _tri()
    dts = [dt_ref[pl.ds(c * CHUNK, CHUNK), :] for c in range(n_chunks)]
    acs_b = _cumsum_rows([d * a_row for d in dts], upper.astype(BF16))
    w_exp, tot_exp = _state_weights(acs_b, dts, eb_ref, 0)
    for c in reversed(range(n_chunks)):
        sin_ref[c] = state_ref[...].astype(BF16)
        _chunk_state_update(state_ref, act_ref, c, w_exp, tot_exp)


def _bwd_sweep(xbc, dt, smat, conv_w8, conv_b, alog, eb, sb0, batch, seq_len):
    t = xbc.shape[0]
    rows = SSD_ROWS
    n_blk = seq_len // rows
    cpb = rows // CHUNK
    halo_per_blk = rows // BF16_ROWS
    n_halo = t // BF16_ROWS

    def blk_of(b, j):
        return b * n_blk + (n_blk - 1 - j)

    return pl.pallas_call(
        functools.partial(_bwd_sweep_kernel, n_blk),
        grid=(batch, n_blk),
        in_specs=[pl.BlockSpec((rows, CONV_CH), lambda b, j: (blk_of(b, j), 0)),
                  pl.BlockSpec((BF16_ROWS, CONV_CH),
                               lambda b, j: (jnp.maximum(blk_of(b, j) * halo_per_blk - 1, 0), 0)),
                  pl.BlockSpec((BF16_ROWS, CONV_CH),
                               lambda b, j: (jnp.minimum((blk_of(b, j) + 1) * halo_per_blk, n_halo - 1), 0)),
                  pl.BlockSpec((rows, DT_W), lambda b, j: (blk_of(b, j), 0)),
                  _const_spec(smat.shape), _const_spec((SUBLANES, CONV_CH)), _const_spec((1, CONV_CH)),
                  _const_spec((1, LANES)), _const_spec(eb.shape),
                  pl.BlockSpec((None, N_SSM_GROUPS, D_STATE, GROUP_W), lambda b, j: (b, 0, 0, 0))],
        out_specs=[pl.BlockSpec((rows, CONV_CH), lambda b, j: (blk_of(b, j), 0)),
                   pl.BlockSpec((cpb, N_SSM_GROUPS, D_STATE, GROUP_W), lambda b, j: (blk_of(b, j), 0, 0, 0))],
        out_shape=[jax.ShapeDtypeStruct((t, CONV_CH), BF16),
                   jax.ShapeDtypeStruct((t // CHUNK, N_SSM_GROUPS, D_STATE, GROUP_W), BF16)],
        scratch_shapes=_conv_scratch(rows) + [pltpu.VMEM((N_SSM_GROUPS, D_STATE, GROUP_W), F32)],
        compiler_params=_cparams(("parallel", "arbitrary")),
        name="ssd_backward_sweep",
    )(xbc, xbc, xbc, dt, smat, conv_w8, conv_b, alog, eb, sb0)


def _fwd_sweep_kernel(act_ref, dt_ref, z_ref, sbin_ref, sf0_ref, alog_ref, ef2_ref, dskip_ref, ng_ref,
                      y_ref, state_ref, ybuf_ref):
    j = pl.program_id(1)

    @pl.when(j == 0)
    def _():
        state_ref[...] = sf0_ref[...]

    a_row = _neg_a_log2(alog_ref)
    lower, upper = _tri()
    lower_bf16 = lower.astype(BF16)
    upper_bf16 = upper.astype(BF16)
    diag = lower & upper
    lane = lax.broadcasted_iota(jnp.int32, (1, LANES), 1)
    lane_lo = lane < SSM_HEAD_DIM
    heads_per_group = N_SSM_HEADS // N_SSM_GROUPS

    n_chunks = SSD_ROWS // CHUNK
    dts = [dt_ref[pl.ds(c * CHUNK, CHUNK), :] for c in range(n_chunks)]
    adts = [d * a_row for d in dts]
    acs_f_all = _cumsum_rows(adts, lower_bf16)
    acs_b_all = _cumsum_rows(adts, upper_bf16)
    adt_ts = [a.T for a in adts]
    acst_f_all = _cumsum_cols(adt_ts, upper_bf16)
    acst_b_all = _cumsum_cols(adt_ts, lower_bf16)
    w_all, tot_all = _state_weights(acs_f_all, dts, ef2_ref, CHUNK - 1)

    for c in range(n_chunks):
        rows = pl.ds(c * CHUNK, CHUNK)
        crows = slice(c * CHUNK, (c + 1) * CHUNK)
        acs_f, acs_b = acs_f_all[c], acs_b_all[c]
        dt_t = dts[c].T
        log2_dt_t = jnp.log2(dt_t)
        row_f = acst_f_all[c] - log2_dt_t
        row_b = acst_b_all[c] - log2_dt_t
        diag_log = jnp.log2(dt_t[0:N_SSM_HEADS] + dt_t[N_SSM_HEADS:2 * N_SSM_HEADS])

        b_chunk = act_ref[rows, D_INNER:D_INNER + BC_W]
        c_chunk = act_ref[rows, D_INNER + BC_W:CONV_CH]

        for g in range(N_SSM_GROUPS):
            c_g = c_chunk[:, g * D_STATE:(g + 1) * D_STATE]
            cb = _dot_nt(c_g, b_chunk[:, g * D_STATE:(g + 1) * D_STATE])
            c_g32 = c_g.astype(F32)
            for pair in range(g * heads_per_group // 2, (g + 1) * heads_per_group // 2):
                pcols = slice((pair % (heads_per_group // 2)) * LANES, (pair % (heads_per_group // 2) + 1) * LANES)
                xs_pair = act_ref[rows, pair * LANES:(pair + 1) * LANES]
                sf_pair = state_ref[g, :, pcols].astype(BF16)
                sb_pair = sbin_ref[c, g, :, pcols]
                outs = []
                for h in (2 * pair, 2 * pair + 1):
                    hb = N_SSM_HEADS + h
                    col_f = jnp.broadcast_to(acs_f[:, h:h + 1], (CHUNK, CHUNK))
                    col_b = jnp.broadcast_to(acs_b[:, hb:hb + 1], (CHUNK, CHUNK))
                    seg = jnp.where(diag, diag_log[h:h + 1, :],
                                    jnp.where(lower, col_f - row_f[h:h + 1, :], col_b - row_b[hb:hb + 1, :]))
                    outs.append(_dot((cb * jnp.exp2(seg)).astype(BF16), xs_pair)
                                + _dot((c_g32 * jnp.exp2(col_f)).astype(BF16), sf_pair)
                                + _dot((c_g32 * jnp.exp2(col_b)).astype(BF16), sb_pair))
                ybuf_ref[:, pair * LANES:(pair + 1) * LANES] = jnp.where(lane_lo, outs[0], outs[1])

        for g in range(N_SSM_GROUPS):
            cols = slice(g * GROUP_W, (g + 1) * GROUP_W)
            y = ybuf_ref[:, cols] + dskip_ref[:, cols] * act_ref[rows, cols].astype(F32)
            y = y * z_ref[rows, cols].astype(F32)
            ms = jnp.mean(y * y, axis=-1, keepdims=True)
            y_ref[rows, cols] = (y * lax.rsqrt(ms + EPS) * ng_ref[:, cols]).astype(BF16)

        _state_update(state_ref, b_chunk, act_ref[rows, 0:D_INNER], w_all[crows], tot_all[c:c + 1])


def _fwd_sweep(act, dt, z, sb_in, sf0, alog, ef2, dskip, norm_g, batch, seq_len):
    t = act.shape[0]
    rows = SSD_ROWS
    n_blk = seq_len // rows
    cpb = rows // CHUNK
    blk = lambda b, j: b * n_blk + j
    return pl.pallas_call(
        _fwd_sweep_kernel,
        grid=(batch, n_blk),
        in_specs=[pl.BlockSpec((rows, CONV_CH), lambda b, j: (blk(b, j), 0)),
                  pl.BlockSpec((rows, DT_W), lambda b, j: (blk(b, j), 0)),
                  pl.BlockSpec((rows, D_INNER), lambda b, j: (blk(b, j), 0)),
                  pl.BlockSpec((cpb, N_SSM_GROUPS, D_STATE, GROUP_W), lambda b, j: (blk(b, j), 0, 0, 0)),
                  pl.BlockSpec((None, N_SSM_GROUPS, D_STATE, GROUP_W), lambda b, j: (b, 0, 0, 0)),
                  _const_spec((1, LANES)), _const_spec(ef2.shape),
                  _const_spec((1, D_INNER)), _const_spec((1, D_INNER))],
        out_specs=pl.BlockSpec((rows, D_INNER), lambda b, j: (blk(b, j), 0)),
        out_shape=jax.ShapeDtypeStruct((t, D_INNER), BF16),
        scratch_shapes=[pltpu.VMEM((N_SSM_GROUPS, D_STATE, GROUP_W), F32),
                        pltpu.VMEM((CHUNK, D_INNER), F32)],
        compiler_params=_cparams(("parallel", "arbitrary")),
        name="ssd_forward_sweep",
    )(act, dt, z, sb_in, sf0, alog, ef2, dskip, norm_g)


FF_SPLIT = 1280


def _merge_ffn_kernel(x_ref, ya_ref, ys_ref, ga_ref, gb_ref, g1_ref, sc_ref, sh_ref, g2_ref, ng_ref, fg_ref,
                      woa_ref, wob_ref, wout_ref, win_ref, wffo_ref, o_ref):
    half_rows = ROW_TILE // 2
    halves = [slice(k * half_rows, (k + 1) * half_rows) for k in range(2)]

    def merge(r):
        a = _dot(ya_ref[r, :], woa_ref[...])
        b = _dot(ys_ref[r, :], wob_ref[...])
        merged = (ga_ref[r, :].astype(F32) * a + gb_ref[r, :].astype(F32) * b).astype(BF16)
        return x_ref[r, :] + g1_ref[...] * _dot(merged, wout_ref[...])

    def swiglu(h):
        acc = None
        for c0, c1 in ((0, FF_SPLIT), (FF_SPLIT, D_FF)):
            gate = _dot(h, win_ref[:, c0:c1])
            up = _dot(h, win_ref[:, D_FF + c0:D_FF + c1])
            part = _dot((_silu(gate) * up).astype(BF16), wffo_ref[c0:c1, :])
            acc = part if acc is None else acc + part
        return acc

    x1 = [merge(r) for r in halves]
    hs = [_adaln(v, ng_ref, sc_ref, sh_ref).astype(BF16) for v in x1]
    ff = [swiglu(h) for h in hs]
    for r, v, f in zip(halves, x1, ff):
        x2 = v + g2_ref[...] * f
        ms = jnp.mean(x2 * x2, axis=-1, keepdims=True)
        o_ref[r, :] = x2 * lax.rsqrt(ms + EPS) * fg_ref[...]


def _merge_ffn(x2, y_att, y_ssm, ga, gb, g1, sc2, sh2, g2, norm_g, final_g, w_oa, w_ob, w_out, w_ffn_in, w_ffn_out,
               seq_len):
    t, d = x2.shape
    tm = ROW_TILE
    per_seq = seq_len // tm
    row = lambda w: pl.BlockSpec((tm, w), lambda i: (i, 0))
    mod_spec = pl.BlockSpec((None, 1, d), lambda i: (i // per_seq, 0, 0))
    return pl.pallas_call(
        _merge_ffn_kernel,
        grid=(t // tm,),
        in_specs=[row(d), row(ATT_W), row(D_INNER), row(d), row(d), mod_spec, mod_spec, mod_spec, mod_spec,
                  _const_spec((1, d)), _const_spec((1, d)),
                  _const_spec(w_oa.shape), _const_spec(w_ob.shape), _const_spec(w_out.shape),
                  _const_spec(w_ffn_in.shape), _const_spec(w_ffn_out.shape)],
        out_specs=row(d),
        out_shape=jax.ShapeDtypeStruct((t, d), F32),
        compiler_params=_cparams(("parallel",)),
        name="merge_ffn_final_norm",
    )(x2, y_att, y_ssm, ga, gb, g1, sc2, sh2, g2, norm_g, final_g, w_oa, w_ob, w_out, w_ffn_in, w_ffn_out)


def _rope_tables(seq_len):
    rows = seq_len // GRID_W
    row = jnp.repeat(jnp.arange(rows), GRID_W).astype(F32)
    col = jnp.tile(jnp.arange(GRID_W), rows).astype(F32)
    inv = ROPE_BASE ** (-jnp.arange(ROPE_FREQS, dtype=F32) / ROPE_FREQS)
    ang_r = row[:, None] * inv
    ang_c = col[:, None] * inv
    ang = jnp.concatenate([ang_r, ang_r, ang_c, ang_c] * 2, axis=-1)
    sign = np.where((np.arange(LANES) % (2 * ROPE_FREQS)) < ROPE_FREQS, -1.0, 1.0).astype(np.float32)
    return jnp.cos(ang), jnp.sin(ang) * jnp.asarray(sign)[None, :]


def _expand_matrix(first_row):
    e = np.zeros((LANES, D_INNER), np.float32)
    for h in range(N_SSM_HEADS):
        e[first_row + h, h * SSM_HEAD_DIM:(h + 1) * SSM_HEAD_DIM] = 1.0
    return jnp.asarray(e, dtype=BF16)


def _conv_shift_matrix():
    s = np.zeros((CHUNK, CONV_K * CONV_WIN), np.float32)
    for k in range(CONV_K):
        off = k + BF16_ROWS - CONV_K // 2 if k <= CONV_K // 2 else k - CONV_K // 2
        for t in range(CHUNK):
            s[t, k * CONV_WIN + t + off] = 1.0
    return jnp.asarray(s, dtype=BF16)


def kernel(x, c, ctx, c_ctx, w_mod, b_mod, norm1_g, w_in, attn_sink, conv_w, conv_b, a_log_f, a_log_b,
           dt_bias_f, dt_bias_b, d_skip, ssm_norm_g, w_oa, w_ob, w_out, norm2_g, w_ffn_in, w_ffn_out, final_g):
    batch, seq_len, d = x.shape
    ctx_len = ctx.shape[1]
    assert w_mod.shape[0] == 1 and d == D_MODEL
    assert seq_len % SSD_ROWS == 0 and seq_len % ROW_TILE == 0 and seq_len % GRID_W == 0
    assert ctx_len % CHUNK == 0

    pad_rows = (-(batch + 1)) % SUBLANES
    c_all = jnp.concatenate([c, c_ctx[None, :], jnp.zeros((pad_rows, d), F32)], axis=0)
    layer = lambda p: p.reshape(p.shape[1:])
    mod = _modulation(c_all, layer(w_mod), b_mod[0][None, :])
    mod_b = mod[:batch].reshape(batch, N_MOD, 1, d)
    sh1, sc1, g1, sh2, sc2, g2 = (mod_b[:, k] for k in range(N_MOD))
    mod_c = mod[batch:batch + 1].reshape(1, N_MOD, 1, d)
    sh1c, sc1c = mod_c[:, 0], mod_c[:, 1]

    w = layer(w_in).astype(BF16)
    o = np.cumsum([0, ATT_W, KV_W, KV_W, D_INNER, CONV_CH, N_SSM_HEADS, N_SSM_HEADS, D_MODEL, D_MODEL])
    seg = lambda k: w[:, o[k]:o[k + 1]]
    w_main = jnp.concatenate(
        [seg(0), seg(1), seg(2), seg(3), seg(4), seg(7), seg(8), seg(5), seg(6),
         jnp.zeros((d, DT_W - 2 * N_SSM_HEADS), BF16)], axis=1)
    pad_heads = jnp.zeros((DT_W - 2 * N_SSM_HEADS,), F32)
    dt_bias = jnp.concatenate([dt_bias_f[0], dt_bias_b[0], pad_heads])[None, :]
    alog = jnp.concatenate([a_log_f[0], a_log_b[0], pad_heads])[None, :]
    conv_w8 = jnp.concatenate([conv_w[0], jnp.zeros((SUBLANES - CONV_K, CONV_CH), F32)], axis=0)
    conv_b1 = conv_b[0][None, :]
    cos, sin = _rope_tables(seq_len)
    ef2 = _expand_matrix(0)
    eb2 = _expand_matrix(N_SSM_HEADS)
    g_norm1 = norm1_g[0][None, :]

    x2 = x.reshape(batch * seq_len, d)
    xc2 = ctx.reshape(batch * ctx_len, d)

    kv_c, xbc_c, dt_c = _inproj_context(xc2, sc1c, sh1c, g_norm1, w_main, dt_bias, ctx_len)
    z, xbc, dt, ga, gb, y_att = _inproj_attention(x2, sc1, sh1, g_norm1, w_main, dt_bias, cos, sin, kv_c,
                                                  attn_sink[0], seq_len, ctx_len)

    smat = _conv_shift_matrix()
    sf0, sb0 = _ctx_states(xbc_c, dt_c, smat, conv_w8, conv_b1, alog, ef2, eb2, batch, ctx_len)
    act, sb_in = _bwd_sweep(xbc, dt, smat, conv_w8, conv_b1, alog, eb2, sb0, batch, seq_len)
    dskip = jnp.repeat(d_skip[0], SSM_HEAD_DIM)[None, :]
    y_ssm = _fwd_sweep(act, dt, z, sb_in, sf0, alog, ef2, dskip, ssm_norm_g[0][None, :], batch, seq_len)

    out = _merge_ffn(x2, y_att, y_ssm, ga, gb, g1, sc2, sh2, g2, norm2_g[0][None, :], final_g[None, :],
                     layer(w_oa).astype(BF16), layer(w_ob).astype(BF16), layer(w_out).astype(BF16),
                     layer(w_ffn_in).astype(BF16), layer(w_ffn_out).astype(BF16), seq_len)
    return out.reshape(batch, seq_len, d)
```

```python
import functools

import numpy as np
import jax
import jax.numpy as jnp
from jax import lax
from jax.experimental import pallas as pl
from jax.experimental.pallas import tpu as pltpu

F32 = jnp.float32
BF16 = jnp.bfloat16

D_MODEL = 1024
GRID_W = 64
EPS = 1e-6
HEAD_DIM = 64
N_Q_HEADS = 16
N_KV_HEADS = 4
ATT_W = N_Q_HEADS * HEAD_DIM
KV_W = N_KV_HEADS * HEAD_DIM
WINDOW = 128
ROPE_BASE = 10000.0
ROPE_FREQS = HEAD_DIM // 4
D_INNER = 2 * D_MODEL
SSM_HEAD_DIM = 64
N_SSM_HEADS = D_INNER // SSM_HEAD_DIM
N_SSM_GROUPS = 4
GROUP_W = D_INNER // N_SSM_GROUPS
D_STATE = 128
BC_W = N_SSM_GROUPS * D_STATE
CONV_K = 5
CONV_CH = D_INNER + 2 * BC_W
CHUNK = 128
CONV_WIN = CHUNK + 16
D_FF = -(-8 * D_MODEL // (3 * 256)) * 256
N_MOD = 6

LANES = 128
SUBLANES = 8
BF16_ROWS = 16
VMEM_LIMIT = 56 * 1024 * 1024

OFF_Q = 0
OFF_K = OFF_Q + ATT_W
OFF_V = OFF_K + KV_W
OFF_Z = OFF_V + KV_W
OFF_XBC = OFF_Z + D_INNER
OFF_GA = OFF_XBC + CONV_CH
OFF_GB = OFF_GA + D_MODEL
OFF_DT = OFF_GB + D_MODEL
DT_W = LANES
N_PROJ = OFF_DT + DT_W
KV_CAT = 2 * N_KV_HEADS * LANES
V_OFF = N_KV_HEADS * LANES
LOG2E = 1.4426950408889634

ROW_TILE = 512
SSD_ROWS = 512
ATT_BLOCK = 128


def _cparams(sem):
    return pltpu.CompilerParams(dimension_semantics=sem, vmem_limit_bytes=VMEM_LIMIT)


def _const_spec(shape):
    n = len(shape)
    return pl.BlockSpec(shape, lambda *_: (0,) * n, pipeline_mode=pl.Buffered(1))


def _dot(a, b):
    return jnp.dot(a, b, preferred_element_type=F32)


def _dot_nt(a, b):
    return lax.dot_general(a, b, (((1,), (1,)), ((), ())), preferred_element_type=F32)


def _dot_tn(a, b):
    return lax.dot_general(a, b, (((0,), (0,)), ((), ())), preferred_element_type=F32)


def _sigmoid(v):
    return 1.0 / (1.0 + jnp.exp(-v))


def _silu(v):
    return v * _sigmoid(v)


def _split3(v):
    hi = v.astype(BF16)
    r = v - hi.astype(F32)
    mid = r.astype(BF16)
    lo = (r - mid.astype(F32)).astype(BF16)
    return hi, mid, lo


def _mod_kernel(c_ref, w_ref, b_ref, o_ref):
    act = _silu(c_ref[...])
    o_ref[...] = jnp.dot(act, w_ref[...], preferred_element_type=F32,
                         precision=lax.Precision.HIGHEST) + b_ref[...]


def _modulation(c_all, w_mod, b_mod):
    rows, d = c_all.shape
    n = w_mod.shape[1]
    tn = 1024
    return pl.pallas_call(
        _mod_kernel,
        grid=(n // tn,),
        in_specs=[pl.BlockSpec((rows, d), lambda j: (0, 0)),
                  pl.BlockSpec((d, tn), lambda j: (0, j)),
                  pl.BlockSpec((1, tn), lambda j: (0, j))],
        out_specs=pl.BlockSpec((rows, tn), lambda j: (0, j)),
        out_shape=jax.ShapeDtypeStruct((rows, n), F32),
        compiler_params=_cparams(("arbitrary",)),
        name="modulation",
    )(c_all, w_mod, b_mod)


def _adaln(x, g_ref, sc_ref, sh_ref):
    ms = jnp.mean(x * x, axis=-1, keepdims=True)
    xn = x * lax.rsqrt(ms + EPS)
    return xn * (g_ref[...] * (1.0 + sc_ref[...])) + sh_ref[...]


def _store_padded_heads(val, col0, kv_ref):
    lane = lax.broadcasted_iota(jnp.int32, (1, LANES), 1)
    is_lo = lane < HEAD_DIM
    for s in range(2):
        slab = val[:, s * LANES:(s + 1) * LANES]
        swapped = pltpu.roll(slab, HEAD_DIM, 1)
        for h, src in ((2 * s, slab), (2 * s + 1, swapped)):
            kv_ref[:, col0 + h * LANES: col0 + (h + 1) * LANES] = jnp.where(is_lo, src, 0.0).astype(BF16)


def _rope_slab(slab, cos, sin_signed, first_half):
    fwd = pltpu.roll(slab, LANES - ROPE_FREQS, 1)
    bwd = pltpu.roll(slab, ROPE_FREQS, 1)
    return slab * cos + jnp.where(first_half, fwd, bwd) * sin_signed


PROJ_PIECE = 1024


def _softplus(v):
    return jnp.maximum(v, 0.0) + jnp.log1p(jnp.exp(-jnp.abs(v)))


def _project_kv(proj, rope, kv_ref):
    k = proj(OFF_K, KV_W)
    if rope is not None:
        k = jnp.concatenate([_rope_slab(k[:, s * LANES:(s + 1) * LANES], *rope) for s in range(2)], axis=1)
    _store_padded_heads(k, 0, kv_ref)
    _store_padded_heads(proj(OFF_V, KV_W), V_OFF, kv_ref)


def _inproj_ctx_kernel(x_ref, sc_ref, sh_ref, g_ref, w_ref, dtb_ref, kv_ref, xbc_ref, dt_ref):
    h = _adaln(x_ref[...], g_ref, sc_ref, sh_ref).astype(BF16)
    proj = lambda off, width: _dot(h, w_ref[:, off:off + width])
    _project_kv(proj, None, kv_ref)
    for c0 in range(0, CONV_CH, PROJ_PIECE):
        xbc_ref[:, c0:c0 + PROJ_PIECE] = proj(OFF_XBC + c0, PROJ_PIECE).astype(BF16)
    dt_ref[...] = _softplus(proj(OFF_DT, DT_W) + dtb_ref[...])


def _inproj_context(xc2, sc, sh, g, w_main, dt_bias, ctx_len):
    t, d = xc2.shape
    tm = ctx_len
    mod_spec = pl.BlockSpec((None, 1, d), lambda i: (0, 0, 0))
    row = lambda w: pl.BlockSpec((tm, w), lambda i: (i, 0))
    sds = lambda w, dt: jax.ShapeDtypeStruct((t, w), dt)
    return pl.pallas_call(
        _inproj_ctx_kernel,
        grid=(t // tm,),
        in_specs=[row(d), mod_spec, mod_spec, _const_spec((1, d)), _const_spec((d, N_PROJ)),
                  _const_spec((1, DT_W))],
        out_specs=[row(KV_CAT), row(CONV_CH), row(DT_W)],
        out_shape=[sds(KV_CAT, BF16), sds(CONV_CH, BF16), sds(DT_W, F32)],
        compiler_params=_cparams(("parallel",)),
        name="inproj_context",
    )(xc2, sc, sh, g, w_main, dt_bias)


def _attn_scores(q_slabs, k_parts):
    half = HEAD_DIM

    def swap_halves(t):
        return jnp.concatenate([t[half:], t[:half]], axis=0)

    t0, t1 = (q.astype(F32).T.astype(BF16) for q in q_slabs)
    q_t = jnp.concatenate([t0, swap_halves(t0), t1, swap_halves(t1)], axis=1)
    return _dot(jnp.concatenate(k_parts, axis=0), q_t)


def _attn_finish(s, v_parts, bias_prev, bias_next, sink_row):
    half = HEAD_DIM
    s_p = s[0:ATT_BLOCK] + bias_prev
    s_c = s[ATT_BLOCK:2 * ATT_BLOCK]
    s_n = s[2 * ATT_BLOCK:3 * ATT_BLOCK] + bias_next
    s_x = s[3 * ATT_BLOCK:]
    red = lambda f, parts: functools.reduce(f, parts)
    m = red(jnp.maximum, [jnp.max(p, axis=0, keepdims=True) for p in (s_p, s_c, s_n, s_x)])
    m = jnp.maximum(m, sink_row)
    e = [jnp.exp2(p - m) for p in (s_p, s_c, s_n, s_x)]
    denom = red(jnp.add, [jnp.sum(p, axis=0, keepdims=True) for p in e]) + jnp.exp2(sink_row - m)
    p_all = jnp.concatenate([p.astype(BF16) for p in e], axis=0)
    v_t = jnp.concatenate(v_parts, axis=0).T[:half]
    out_t = _dot(v_t, p_all) * (1.0 / denom)
    return [jnp.concatenate([out_t[:, (2 * sl) * ATT_BLOCK:(2 * sl + 1) * ATT_BLOCK],
                             out_t[:, (2 * sl + 1) * ATT_BLOCK:(2 * sl + 2) * ATT_BLOCK]], axis=0).T
            for sl in range(2)]


def _inproj_attn_kernel(per_seq, sink_ref, x_ref, sc_ref, sh_ref, g_ref, w_ref, dtb_ref, cos_ref, sin_ref, kvx_ref,
                        z_ref, xbc_ref, dt_ref, ga_ref, gb_ref, ya_ref,
                        q_cur, q_nxt, kv_cur, kv_nxt, kv_tail):
    i = pl.program_id(0)
    blocks = ROW_TILE // ATT_BLOCK
    group = N_Q_HEADS // N_KV_HEADS
    width = group * ATT_BLOCK

    @pl.when(i == 0)
    def _():
        q_cur[...] = jnp.zeros(q_cur.shape, BF16)
        kv_cur[...] = jnp.zeros(kv_cur.shape, BF16)
        kv_tail[...] = jnp.zeros(kv_tail.shape, BF16)

    def parts(sub, kvh, col0):
        qrows = slice(sub * ATT_BLOCK, (sub + 1) * ATT_BLOCK)
        cols = slice(col0 + kvh * LANES, col0 + (kvh + 1) * LANES)
        prev = kv_tail[:, cols] if sub == 0 else kv_cur[(sub - 1) * ATT_BLOCK:sub * ATT_BLOCK, cols]
        nxt = (kv_nxt[0:ATT_BLOCK, cols] if sub == blocks - 1
               else kv_cur[(sub + 1) * ATT_BLOCK:(sub + 2) * ATT_BLOCK, cols])
        return [prev, kv_cur[qrows, cols], nxt, kvx_ref[:, cols]]

    def scores(sub, kvh):
        qrows = slice(sub * ATT_BLOCK, (sub + 1) * ATT_BLOCK)
        q_slabs = [q_cur[qrows, (2 * kvh + sl) * LANES:(2 * kvh + sl + 1) * LANES] for sl in range(2)]
        return _attn_scores(q_slabs, parts(sub, kvh, 0))

    work = [(sub, kvh) for sub in range(blocks) for kvh in range(N_KV_HEADS)]
    s_next = scores(*work[0])

    h = _adaln(x_ref[...], g_ref, sc_ref, sh_ref).astype(BF16)
    proj = lambda off, wid: _dot(h, w_ref[:, off:off + wid])
    cos = cos_ref[...]
    sin = sin_ref[...]
    lane = lax.broadcasted_iota(jnp.int32, (1, LANES), 1)
    first_half = (lane % (2 * ROPE_FREQS)) < ROPE_FREQS
    _project_kv(proj, (cos, sin, first_half), kv_nxt)
    scale = HEAD_DIM ** -0.5 * LOG2E
    q = proj(OFF_Q, ATT_W)
    for s in range(ATT_W // LANES):
        q_nxt[:, s * LANES:(s + 1) * LANES] = (
            _rope_slab(q[:, s * LANES:(s + 1) * LANES], cos, sin, first_half) * scale).astype(BF16)

    piece = PROJ_PIECE // 2

    def store_xbc(c0, wid=piece):
        xbc_ref[:, c0:c0 + wid] = proj(OFF_XBC + c0, wid).astype(BF16)

    def store_z(c0):
        z_ref[:, c0:c0 + piece] = _silu(proj(OFF_Z + c0, piece)).astype(BF16)

    def store_dt():
        dt_ref[...] = _softplus(proj(OFF_DT, DT_W) + dtb_ref[...])

    def store_gate(ref, off, c0):
        ref[:, c0:c0 + piece] = _sigmoid(proj(off + c0, piece)).astype(BF16)

    plain = ([functools.partial(store_xbc, c0) for c0 in range(0, CONV_CH - piece, piece)]
             + [functools.partial(store_xbc, CONV_CH - piece, piece // 2),
                functools.partial(store_xbc, CONV_CH - piece // 2, piece // 2), store_dt])
    gated = ([functools.partial(store_z, c0) for c0 in range(0, D_INNER, piece)]
             + [functools.partial(store_gate, ga_ref, OFF_GA, c0) for c0 in range(0, D_MODEL, piece)]
             + [functools.partial(store_gate, gb_ref, OFF_GB, c0) for c0 in range(0, D_MODEL, piece)])
    segments = [s for pair in zip(plain, gated) for s in pair] + gated[len(plain):] + plain[len(gated):]

    seq_tile = (i + per_seq - 1) % per_seq
    key_j = lax.broadcasted_iota(jnp.int32, (ATT_BLOCK, width), 0)
    lane_w = lax.broadcasted_iota(jnp.int32, (ATT_BLOCK, width), 1)
    query_r = lane_w % ATT_BLOCK
    neg = jnp.float32(-1e30)
    band_prev = key_j >= query_r
    band_next = key_j <= query_r
    lane1 = lax.broadcasted_iota(jnp.int32, (1, width), 1)

    for n, (sub, kvh) in enumerate(work):
        s_cur = s_next
        if n + 1 < len(work):
            s_next = scores(*work[n + 1])
        if segments:
            segments.pop(0)()
        qrows = slice(sub * ATT_BLOCK, (sub + 1) * ATT_BLOCK)
        prev_ok = (seq_tile > 0) if sub == 0 else True
        next_ok = (seq_tile < per_seq - 1) if sub == blocks - 1 else True
        bias_prev = jnp.where(band_prev & prev_ok, 0.0, neg)
        bias_next = jnp.where(band_next & next_ok, 0.0, neg)
        sink_row = jnp.zeros((1, width), F32)
        for g in range(group):
            sink_row = jnp.where(lane1 // ATT_BLOCK == g, sink_ref[kvh * group + g] * LOG2E, sink_row)
        outs = _attn_finish(s_cur, parts(sub, kvh, V_OFF), bias_prev, bias_next, sink_row)
        for sl in range(2):
            ya_ref[qrows, (2 * kvh + sl) * LANES:(2 * kvh + sl + 1) * LANES] = outs[sl].astype(BF16)
    for seg in segments:
        seg()

    kv_tail[...] = kv_cur[ROW_TILE - ATT_BLOCK:, :]
    kv_cur[...] = kv_nxt[...]
    q_cur[...] = q_nxt[...]


def _inproj_attention(x2, sc, sh, g, w_main, dt_bias, cos, sin, kv_ctx, sink, seq_len, ctx_len):
    t, d = x2.shape
    tm = ROW_TILE
    per_seq = seq_len // tm
    n_tiles = t // tm
    cur = lambda i: jnp.minimum(i, n_tiles - 1)
    prev = lambda i: jnp.maximum(i - 1, 0)
    mod_spec = pl.BlockSpec((None, 1, d), lambda i: (cur(i) // per_seq, 0, 0))
    pos_spec = pl.BlockSpec((tm, LANES), lambda i: (cur(i) % per_seq, 0))
    row = lambda w: pl.BlockSpec((tm, w), lambda i: (cur(i), 0))
    sds = lambda w, dt: jax.ShapeDtypeStruct((t, w), dt)
    return pl.pallas_call(
        functools.partial(_inproj_attn_kernel, per_seq),
        grid=(n_tiles + 1,),
        in_specs=[pl.BlockSpec(memory_space=pltpu.SMEM), row(d), mod_spec, mod_spec, _const_spec((1, d)),
                  _const_spec((d, N_PROJ)), _const_spec((1, DT_W)), pos_spec, pos_spec,
                  pl.BlockSpec((ctx_len, KV_CAT), lambda i: (prev(i) // per_seq, 0))],
        out_specs=[row(D_INNER), row(CONV_CH), row(DT_W), row(d), row(d),
                   pl.BlockSpec((tm, ATT_W), lambda i: (prev(i), 0))],
        out_shape=[sds(D_INNER, BF16), sds(CONV_CH, BF16), sds(DT_W, F32), sds(d, BF16), sds(d, BF16),
                   sds(ATT_W, BF16)],
        scratch_shapes=[pltpu.VMEM((tm, ATT_W), BF16), pltpu.VMEM((tm, ATT_W), BF16),
                        pltpu.VMEM((tm, KV_CAT), BF16), pltpu.VMEM((tm, KV_CAT), BF16),
                        pltpu.VMEM((ATT_BLOCK, KV_CAT), BF16)],
        compiler_params=_cparams(("arbitrary",)),
        name="inproj_attention",
    )(sink, x2, sc, sh, g, w_main, dt_bias, cos, sin, kv_ctx)


def _conv_silu(buf_ref, n_blocks, smat_ref, w_ref, b_ref, p_ref, out_ref):
    for m in range(n_blocks):
        r0 = m * CHUNK
        slot = m % 2
        for k in range(CONV_K):
            start = r0 if k <= CONV_K // 2 else r0 + BF16_ROWS
            p_ref[slot, k * CONV_WIN:(k + 1) * CONV_WIN, :] = (
                buf_ref[pl.ds(start, CONV_WIN), :] * w_ref[k:k + 1, :].astype(BF16))
        acc = _dot(smat_ref[...], p_ref[slot]) + b_ref[...]
        out_ref[pl.ds(r0, CHUNK), :] = _silu(acc).astype(out_ref.dtype)


def _tri():
    r = lax.broadcasted_iota(jnp.int32, (CHUNK, CHUNK), 0)
    c = lax.broadcasted_iota(jnp.int32, (CHUNK, CHUNK), 1)
    return r >= c, r <= c


def _cumsum_rows(adts, tri_bf16):
    x3 = jnp.concatenate([t for a in adts for t in _split3(a)], axis=1)
    c = _dot(tri_bf16, x3)
    w = 3 * LANES
    return [c[:, i * w:i * w + LANES] + c[:, i * w + LANES:i * w + 2 * LANES] + c[:, i * w + 2 * LANES:(i + 1) * w]
            for i in range(len(adts))]


def _cumsum_cols(adt_ts, tri_bf16):
    x3 = jnp.concatenate([t for a in adt_ts for t in _split3(a)], axis=0)
    c = _dot(x3, tri_bf16)
    h = 3 * CHUNK
    return [c[i * h:i * h + CHUNK] + c[i * h + CHUNK:i * h + 2 * CHUNK] + c[i * h + 2 * CHUNK:(i + 1) * h]
            for i in range(len(adt_ts))]


def _expand(v, e_ref):
    return _dot(v.astype(BF16), e_ref[...])


def _state_update(state_ref, b_chunk, xs_chunk, w_exp, tot_exp):
    for g in range(N_SSM_GROUPS):
        cols = slice(g * GROUP_W, (g + 1) * GROUP_W)
        xw = (xs_chunk[:, cols].astype(F32) * w_exp[:, cols]).astype(BF16)
        upd = _dot_tn(b_chunk[:, g * D_STATE:(g + 1) * D_STATE], xw)
        state_ref[g] = state_ref[g] * tot_exp[:, cols] + upd


def _neg_a_log2(alog_ref):
    return -jnp.exp(alog_ref[...]) * LOG2E


def _state_weights(acs, dts, e_ref, total_row):
    totals = [a[total_row:total_row + 1, :] for a in acs]
    w = jnp.concatenate([jnp.exp2(t - a) * d for t, a, d in zip(totals, acs, dts)], axis=0)
    pad = jnp.zeros((SUBLANES - len(acs), LANES), F32)
    tot = jnp.concatenate([jnp.exp2(t) for t in totals] + [pad], axis=0)
    return _expand(w, e_ref), _expand(tot, e_ref)


def _chunk_state_update(state_ref, act_ref, c, w_exp, tot_exp):
    rows = pl.ds(c * CHUNK, CHUNK)
    _state_update(state_ref, act_ref[rows, D_INNER:D_INNER + BC_W], act_ref[rows, 0:D_INNER],
                  w_exp[c * CHUNK:(c + 1) * CHUNK], tot_exp[c:c + 1])


def _ctx_state_kernel(n_chunks, xbc_ref, dt_ref, smat_ref, cw_ref, cb_ref, alog_ref, ef_ref, eb_ref,
                      sf_ref, sb_ref, buf_ref, p_ref, act_ref):
    rows = n_chunks * CHUNK
    zeros = jnp.zeros((BF16_ROWS, CONV_CH), BF16)
    buf_ref[0:BF16_ROWS, :] = zeros
    buf_ref[pl.ds(BF16_ROWS, rows), :] = xbc_ref[...]
    buf_ref[pl.ds(BF16_ROWS + rows, BF16_ROWS), :] = zeros
    _conv_silu(buf_ref, n_chunks, smat_ref, cw_ref, cb_ref, p_ref, act_ref)

    a_row = _neg_a_log2(alog_ref)
    lower, upper = _tri()
    dts = [dt_ref[pl.ds(c * CHUNK, CHUNK), :] for c in range(n_chunks)]
    adts = [d * a_row for d in dts]
    wf_exp, totf_exp = _state_weights(_cumsum_rows(adts, lower.astype(BF16)), dts, ef_ref, CHUNK - 1)
    wb_exp, totb_exp = _state_weights(_cumsum_rows(adts, upper.astype(BF16)), dts, eb_ref, 0)
    sf_ref[...] = jnp.zeros(sf_ref.shape, F32)
    sb_ref[...] = jnp.zeros(sb_ref.shape, F32)
    for c in range(n_chunks):
        _chunk_state_update(sf_ref, act_ref, c, wf_exp, totf_exp)
    for c in reversed(range(n_chunks)):
        _chunk_state_update(sb_ref, act_ref, c, wb_exp, totb_exp)


def _conv_scratch(rows):
    return [pltpu.VMEM((rows + 2 * BF16_ROWS, CONV_CH), BF16),
            pltpu.VMEM((2, CONV_K * CONV_WIN, CONV_CH), BF16)]


def _ctx_states(xbc_c, dt_c, smat, conv_w8, conv_b, alog, ef, eb, batch, ctx_len):
    n_chunks = ctx_len // CHUNK
    state = jax.ShapeDtypeStruct((batch, N_SSM_GROUPS, D_STATE, GROUP_W), F32)
    st_spec = pl.BlockSpec((None, N_SSM_GROUPS, D_STATE, GROUP_W), lambda b: (b, 0, 0, 0))
    return pl.pallas_call(
        functools.partial(_ctx_state_kernel, n_chunks),
        grid=(batch,),
        in_specs=[pl.BlockSpec((ctx_len, CONV_CH), lambda b: (b, 0)),
                  pl.BlockSpec((ctx_len, DT_W), lambda b: (b, 0)),
                  _const_spec(smat.shape), _const_spec((SUBLANES, CONV_CH)), _const_spec((1, CONV_CH)),
                  _const_spec((1, LANES)), _const_spec(ef.shape), _const_spec(eb.shape)],
        out_specs=[st_spec, st_spec],
        out_shape=[state, state],
        scratch_shapes=_conv_scratch(ctx_len) + [pltpu.VMEM((ctx_len, CONV_CH), BF16)],
        compiler_params=_cparams(("parallel",)),
        name="ssd_context_states",
    )(xbc_c, dt_c, smat, conv_w8, conv_b, alog, ef, eb)


def _bwd_sweep_kernel(n_blk, cur_ref, prev_ref, next_ref, dt_ref, smat_ref, cw_ref, cb_ref, alog_ref, eb_ref,
                      sb0_ref, act_ref, sin_ref, buf_ref, p_ref, state_ref):
    j = pl.program_id(1)
    blk = n_blk - 1 - j
    rows = SSD_ROWS
    n_chunks = rows // CHUNK

    @pl.when(j == 0)
    def _():
        state_ref[...] = sb0_ref[...]

    zeros = jnp.zeros((BF16_ROWS, CONV_CH), BF16)
    buf_ref[0:BF16_ROWS, :] = jnp.where(blk > 0, prev_ref[...], zeros)
    buf_ref[pl.ds(BF16_ROWS, rows), :] = cur_ref[...]
    buf_ref[pl.ds(BF16_ROWS + rows, BF16_ROWS), :] = jnp.where(blk < n_blk - 1, next_ref[...], zeros)
    _conv_silu(buf_ref, n_chunks, smat_ref, cw_ref, cb_ref, p_ref, act_ref)

    a_row = _neg_a_log2(alog_ref)
    _, upper = _tri()
    dts = [dt_ref[pl.ds(c * CHUNK, CHUNK), :] for c in range(n_chunks)]
    acs_b = _cumsum_rows([d * a_row for d in dts], upper.astype(BF16))
    w_exp, tot_exp = _state_weights(acs_b, dts, eb_ref, 0)
    for c in reversed(range(n_chunks)):
        sin_ref[c] = state_ref[...].astype(BF16)
        _chunk_state_update(state_ref, act_ref, c, w_exp, tot_exp)


def _bwd_sweep(xbc, dt, smat, conv_w8, conv_b, alog, eb, sb0, batch, seq_len):
    t = xbc.shape[0]
    rows = SSD_ROWS
    n_blk = seq_len // rows
    cpb = rows // CHUNK
    halo_per_blk = rows // BF16_ROWS
    n_halo = t // BF16_ROWS

    def blk_of(b, j):
        return b * n_blk + (n_blk - 1 - j)

    return pl.pallas_call(
        functools.partial(_bwd_sweep_kernel, n_blk),
        grid=(batch, n_blk),
        in_specs=[pl.BlockSpec((rows, CONV_CH), lambda b, j: (blk_of(b, j), 0)),
                  pl.BlockSpec((BF16_ROWS, CONV_CH),
                               lambda b, j: (jnp.maximum(blk_of(b, j) * halo_per_blk - 1, 0), 0)),
                  pl.BlockSpec((BF16_ROWS, CONV_CH),
                               lambda b, j: (jnp.minimum((blk_of(b, j) + 1) * halo_per_blk, n_halo - 1), 0)),
                  pl.BlockSpec((rows, DT_W), lambda b, j: (blk_of(b, j), 0)),
                  _const_spec(smat.shape), _const_spec((SUBLANES, CONV_CH)), _const_spec((1, CONV_CH)),
                  _const_spec((1, LANES)), _const_spec(eb.shape),
                  pl.BlockSpec((None, N_SSM_GROUPS, D_STATE, GROUP_W), lambda b, j: (b, 0, 0, 0))],
        out_specs=[pl.BlockSpec((rows, CONV_CH), lambda b, j: (blk_of(b, j), 0)),
                   pl.BlockSpec((cpb, N_SSM_GROUPS, D_STATE, GROUP_W), lambda b, j: (blk_of(b, j), 0, 0, 0))],
        out_shape=[jax.ShapeDtypeStruct((t, CONV_CH), BF16),
                   jax.ShapeDtypeStruct((t // CHUNK, N_SSM_GROUPS, D_STATE, GROUP_W), BF16)],
        scratch_shapes=_conv_scratch(rows) + [pltpu.VMEM((N_SSM_GROUPS, D_STATE, GROUP_W), F32)],
        compiler_params=_cparams(("parallel", "arbitrary")),
        name="ssd_backward_sweep",
    )(xbc, xbc, xbc, dt, smat, conv_w8, conv_b, alog, eb, sb0)


def _fwd_sweep_kernel(act_ref, dt_ref, z_ref, sbin_ref, sf0_ref, alog_ref, ef2_ref, dskip_ref, ng_ref,
                      y_ref, state_ref, ybuf_ref):
    j = pl.program_id(1)

    @pl.when(j == 0)
    def _():
        state_ref[...] = sf0_ref[...]

    a_row = _neg_a_log2(alog_ref)
    lower, upper = _tri()
    lower_bf16 = lower.astype(BF16)
    upper_bf16 = upper.astype(BF16)
    diag = lower & upper
    lane = lax.broadcasted_iota(jnp.int32, (1, LANES), 1)
    lane_lo = lane < SSM_HEAD_DIM
    heads_per_group = N_SSM_HEADS // N_SSM_GROUPS

    n_chunks = SSD_ROWS // CHUNK
    dts = [dt_ref[pl.ds(c * CHUNK, CHUNK), :] for c in range(n_chunks)]
    adts = [d * a_row for d in dts]
    acs_f_all = _cumsum_rows(adts, lower_bf16)
    acs_b_all = _cumsum_rows(adts, upper_bf16)
    adt_ts = [a.T for a in adts]
    acst_f_all = _cumsum_cols(adt_ts, upper_bf16)
    acst_b_all = _cumsum_cols(adt_ts, lower_bf16)
    w_all, tot_all = _state_weights(acs_f_all, dts, ef2_ref, CHUNK - 1)

    for c in range(n_chunks):
        rows = pl.ds(c * CHUNK, CHUNK)
        crows = slice(c * CHUNK, (c + 1) * CHUNK)
        acs_f, acs_b = acs_f_all[c], acs_b_all[c]
        dt_t = dts[c].T
        log2_dt_t = jnp.log2(dt_t)
        row_f = acst_f_all[c] - log2_dt_t
        row_b = acst_b_all[c] - log2_dt_t
        diag_log = jnp.log2(dt_t[0:N_SSM_HEADS] + dt_t[N_SSM_HEADS:2 * N_SSM_HEADS])

        b_chunk = act_ref[rows, D_INNER:D_INNER + BC_W]
        c_chunk = act_ref[rows, D_INNER + BC_W:CONV_CH]

        for g in range(N_SSM_GROUPS):
            c_g = c_chunk[:, g * D_STATE:(g + 1) * D_STATE]
            cb = _dot_nt(c_g, b_chunk[:, g * D_STATE:(g + 1) * D_STATE])
            c_g32 = c_g.astype(F32)
            for pair in range(g * heads_per_group // 2, (g + 1) * heads_per_group // 2):
                pcols = slice((pair % (heads_per_group // 2)) * LANES, (pair % (heads_per_group // 2) + 1) * LANES)
                xs_pair = act_ref[rows, pair * LANES:(pair + 1) * LANES]
                sf_pair = state_ref[g, :, pcols].astype(BF16)
                sb_pair = sbin_ref[c, g, :, pcols]
                outs = []
                for h in (2 * pair, 2 * pair + 1):
                    hb = N_SSM_HEADS + h
                    col_f = jnp.broadcast_to(acs_f[:, h:h + 1], (CHUNK, CHUNK))
                    col_b = jnp.broadcast_to(acs_b[:, hb:hb + 1], (CHUNK, CHUNK))
                    seg = jnp.where(diag, diag_log[h:h + 1, :],
                                    jnp.where(lower, col_f - row_f[h:h + 1, :], col_b - row_b[hb:hb + 1, :]))
                    outs.append(_dot((cb * jnp.exp2(seg)).astype(BF16), xs_pair)
                                + _dot((c_g32 * jnp.exp2(col_f)).astype(BF16), sf_pair)
                                + _dot((c_g32 * jnp.exp2(col_b)).astype(BF16), sb_pair))
                ybuf_ref[:, pair * LANES:(pair + 1) * LANES] = jnp.where(lane_lo, outs[0], outs[1])

        for g in range(N_SSM_GROUPS):
            cols = slice(g * GROUP_W, (g + 1) * GROUP_W)
            y = ybuf_ref[:, cols] + dskip_ref[:, cols] * act_ref[rows, cols].astype(F32)
            y = y * z_ref[rows, cols].astype(F32)
            ms = jnp.mean(y * y, axis=-1, keepdims=True)
            y_ref[rows, cols] = (y * lax.rsqrt(ms + EPS) * ng_ref[:, cols]).astype(BF16)

        _state_update(state_ref, b_chunk, act_ref[rows, 0:D_INNER], w_all[crows], tot_all[c:c + 1])


def _fwd_sweep(act, dt, z, sb_in, sf0, alog, ef2, dskip, norm_g, batch, seq_len):
    t = act.shape[0]
    rows = SSD_ROWS
    n_blk = seq_len // rows
    cpb = rows // CHUNK
    blk = lambda b, j: b * n_blk + j
    return pl.pallas_call(
        _fwd_sweep_kernel,
        grid=(batch, n_blk),
        in_specs=[pl.BlockSpec((rows, CONV_CH), lambda b, j: (blk(b, j), 0)),
                  pl.BlockSpec((rows, DT_W), lambda b, j: (blk(b, j), 0)),
                  pl.BlockSpec((rows, D_INNER), lambda b, j: (blk(b, j), 0)),
                  pl.BlockSpec((cpb, N_SSM_GROUPS, D_STATE, GROUP_W), lambda b, j: (blk(b, j), 0, 0, 0)),
                  pl.BlockSpec((None, N_SSM_GROUPS, D_STATE, GROUP_W), lambda b, j: (b, 0, 0, 0)),
                  _const_spec((1, LANES)), _const_spec(ef2.shape),
                  _const_spec((1, D_INNER)), _const_spec((1, D_INNER))],
        out_specs=pl.BlockSpec((rows, D_INNER), lambda b, j: (blk(b, j), 0)),
        out_shape=jax.ShapeDtypeStruct((t, D_INNER), BF16),
        scratch_shapes=[pltpu.VMEM((N_SSM_GROUPS, D_STATE, GROUP_W), F32),
                        pltpu.VMEM((CHUNK, D_INNER), F32)],
        compiler_params=_cparams(("parallel", "arbitrary")),
        name="ssd_forward_sweep",
    )(act, dt, z, sb_in, sf0, alog, ef2, dskip, norm_g)


FF_SPLIT = 1280


def _merge_ffn_kernel(x_ref, ya_ref, ys_ref, ga_ref, gb_ref, g1_ref, sc_ref, sh_ref, g2_ref, ng_ref, fg_ref,
                      woa_ref, wob_ref, wout_ref, win_ref, wffo_ref, o_ref):
    half_rows = ROW_TILE // 2
    halves = [slice(k * half_rows, (k + 1) * half_rows) for k in range(2)]

    def merge(r):
        a = _dot(ya_ref[r, :], woa_ref[...])
        b = _dot(ys_ref[r, :], wob_ref[...])
        merged = (ga_ref[r, :].astype(F32) * a + gb_ref[r, :].astype(F32) * b).astype(BF16)
        return x_ref[r, :] + g1_ref[...] * _dot(merged, wout_ref[...])

    def swiglu(h):
        acc = None
        for c0, c1 in ((0, FF_SPLIT), (FF_SPLIT, D_FF)):
            gate = _dot(h, win_ref[:, c0:c1])
            up = _dot(h, win_ref[:, D_FF + c0:D_FF + c1])
            part = _dot((_silu(gate) * up).astype(BF16), wffo_ref[c0:c1, :])
            acc = part if acc is None else acc + part
        return acc

    x1 = [merge(r) for r in halves]
    hs = [_adaln(v, ng_ref, sc_ref, sh_ref).astype(BF16) for v in x1]
    ff = [swiglu(h) for h in hs]
    for r, v, f in zip(halves, x1, ff):
        x2 = v + g2_ref[...] * f
        ms = jnp.mean(x2 * x2, axis=-1, keepdims=True)
        o_ref[r, :] = x2 * lax.rsqrt(ms + EPS) * fg_ref[...]


def _merge_ffn(x2, y_att, y_ssm, ga, gb, g1, sc2, sh2, g2, norm_g, final_g, w_oa, w_ob, w_out, w_ffn_in, w_ffn_out,
               seq_len):
    t, d = x2.shape
    tm = ROW_TILE
    per_seq = seq_len // tm
    row = lambda w: pl.BlockSpec((tm, w), lambda i: (i, 0))
    mod_spec = pl.BlockSpec((None, 1, d), lambda i: (i // per_seq, 0, 0))
    return pl.pallas_call(
        _merge_ffn_kernel,
        grid=(t // tm,),
        in_specs=[row(d), row(ATT_W), row(D_INNER), row(d), row(d), mod_spec, mod_spec, mod_spec, mod_spec,
                  _const_spec((1, d)), _const_spec((1, d)),
                  _const_spec(w_oa.shape), _const_spec(w_ob.shape), _const_spec(w_out.shape),
                  _const_spec(w_ffn_in.shape), _const_spec(w_ffn_out.shape)],
        out_specs=row(d),
        out_shape=jax.ShapeDtypeStruct((t, d), F32),
        compiler_params=_cparams(("parallel",)),
        name="merge_ffn_final_norm",
    )(x2, y_att, y_ssm, ga, gb, g1, sc2, sh2, g2, norm_g, final_g, w_oa, w_ob, w_out, w_ffn_in, w_ffn_out)


def _rope_tables(seq_len):
    rows = seq_len // GRID_W
    row = jnp.repeat(jnp.arange(rows), GRID_W).astype(F32)
    col = jnp.tile(jnp.arange(GRID_W), rows).astype(F32)
    inv = ROPE_BASE ** (-jnp.arange(ROPE_FREQS, dtype=F32) / ROPE_FREQS)
    ang_r = row[:, None] * inv
    ang_c = col[:, None] * inv
    ang = jnp.concatenate([ang_r, ang_r, ang_c, ang_c] * 2, axis=-1)
    sign = np.where((np.arange(LANES) % (2 * ROPE_FREQS)) < ROPE_FREQS, -1.0, 1.0).astype(np.float32)
    return jnp.cos(ang), jnp.sin(ang) * jnp.asarray(sign)[None, :]


def _expand_matrix(first_row):
    e = np.zeros((LANES, D_INNER), np.float32)
    for h in range(N_SSM_HEADS):
        e[first_row + h, h * SSM_HEAD_DIM:(h + 1) * SSM_HEAD_DIM] = 1.0
    return jnp.asarray(e, dtype=BF16)


def _conv_shift_matrix():
    s = np.zeros((CHUNK, CONV_K * CONV_WIN), np.float32)
    for k in range(CONV_K):
        off = k + BF16_ROWS - CONV_K // 2 if k <= CONV_K // 2 else k - CONV_K // 2
        for t in range(CHUNK):
            s[t, k * CONV_WIN + t + off] = 1.0
    return jnp.asarray(s, dtype=BF16)


def kernel(x, c, ctx, c_ctx, w_mod, b_mod, norm1_g, w_in, attn_sink, conv_w, conv_b, a_log_f, a_log_b,
           dt_bias_f, dt_bias_b, d_skip, ssm_norm_g, w_oa, w_ob, w_out, norm2_g, w_ffn_in, w_ffn_out, final_g):
    batch, seq_len, d = x.shape
    ctx_len = ctx.shape[1]
    assert w_mod.shape[0] == 1 and d == D_MODEL
    assert seq_len % SSD_ROWS == 0 and seq_len % ROW_TILE == 0 and seq_len % GRID_W == 0
    assert ctx_len % CHUNK == 0

    pad_rows = (-(batch + 1)) % SUBLANES
    c_all = jnp.concatenate([c, c_ctx[None, :], jnp.zeros((pad_rows, d), F32)], axis=0)
    layer = lambda p: p.reshape(p.shape[1:])
    mod = _modulation(c_all, layer(w_mod), b_mod[0][None, :])
    mod_b = mod[:batch].reshape(batch, N_MOD, 1, d)
    sh1, sc1, g1, sh2, sc2, g2 = (mod_b[:, k] for k in range(N_MOD))
    mod_c = mod[batch:batch + 1].reshape(1, N_MOD, 1, d)
    sh1c, sc1c = mod_c[:, 0], mod_c[:, 1]

    w = layer(w_in).astype(BF16)
    o = np.cumsum([0, ATT_W, KV_W, KV_W, D_INNER, CONV_CH, N_SSM_HEADS, N_SSM_HEADS, D_MODEL, D_MODEL])
    w_main = jnp.concatenate(
        [w[:, :o[5]], w[:, o[7]:o[9]], w[:, o[5]:o[7]], jnp.zeros((d, DT_W - 2 * N_SSM_HEADS), BF16)], axis=1)
    pad_heads = jnp.zeros((DT_W - 2 * N_SSM_HEADS,), F32)
    dt_bias = jnp.concatenate([dt_bias_f[0], dt_bias_b[0], pad_heads])[None, :]
    alog = jnp.concatenate([a_log_f[0], a_log_b[0], pad_heads])[None, :]
    conv_w8 = jnp.concatenate([conv_w[0], jnp.zeros((SUBLANES - CONV_K, CONV_CH), F32)], axis=0)
    conv_b1 = conv_b[0][None, :]
    cos, sin = _rope_tables(seq_len)
    ef2 = _expand_matrix(0)
    eb2 = _expand_matrix(N_SSM_HEADS)
    g_norm1 = norm1_g[0][None, :]

    x2 = x.reshape(batch * seq_len, d)
    xc2 = ctx.reshape(batch * ctx_len, d)

    kv_c, xbc_c, dt_c = _inproj_context(xc2, sc1c, sh1c, g_norm1, w_main, dt_bias, ctx_len)
    z, xbc, dt, ga, gb, y_att = _inproj_attention(x2, sc1, sh1, g_norm1, w_main, dt_bias, cos, sin, kv_c,
                                                  attn_sink[0], seq_len, ctx_len)

    smat = _conv_shift_matrix()
    sf0, sb0 = _ctx_states(xbc_c, dt_c, smat, conv_w8, conv_b1, alog, ef2, eb2, batch, ctx_len)
    act, sb_in = _bwd_sweep(xbc, dt, smat, conv_w8, conv_b1, alog, eb2, sb0, batch, seq_len)
    dskip = jnp.repeat(d_skip[0], SSM_HEAD_DIM)[None, :]
    y_ssm = _fwd_sweep(act, dt, z, sb_in, sf0, alog, ef2, dskip, ssm_norm_g[0][None, :], batch, seq_len)

    out = _merge_ffn(x2, y_att, y_ssm, ga, gb, g1, sc2, sh2, g2, norm2_g[0][None, :], final_g[None, :],
                     layer(w_oa).astype(BF16), layer(w_ob).astype(BF16), layer(w_out).astype(BF16),
                     layer(w_ffn_in).astype(BF16), layer(w_ffn_out).astype(BF16), seq_len)
    return out.reshape(batch, seq_len, d)
```

```python
import functools

import numpy as np
import jax
import jax.numpy as jnp
from jax import lax
from jax.experimental import pallas as pl
from jax.experimental.pallas import tpu as pltpu

F32 = jnp.float32
BF16 = jnp.bfloat16

D_MODEL = 1024
GRID_W = 64
EPS = 1e-6
HEAD_DIM = 64
N_Q_HEADS = 16
N_KV_HEADS = 4
ATT_W = N_Q_HEADS * HEAD_DIM
KV_W = N_KV_HEADS * HEAD_DIM
WINDOW = 128
ROPE_BASE = 10000.0
ROPE_FREQS = HEAD_DIM // 4
D_INNER = 2 * D_MODEL
SSM_HEAD_DIM = 64
N_SSM_HEADS = D_INNER // SSM_HEAD_DIM
N_SSM_GROUPS = 4
GROUP_W = D_INNER // N_SSM_GROUPS
D_STATE = 128
BC_W = N_SSM_GROUPS * D_STATE
CONV_K = 5
CONV_CH = D_INNER + 2 * BC_W
CHUNK = 128
D_FF = -(-8 * D_MODEL // (3 * 256)) * 256
N_MOD = 6

LANES = 128
SUBLANES = 8
BF16_ROWS = 16
CONV_WIN = CHUNK + BF16_ROWS
VMEM_LIMIT = 56 * 1024 * 1024

OFF_Q = 0
OFF_K = OFF_Q + ATT_W
OFF_V = OFF_K + KV_W
OFF_Z = OFF_V + KV_W
OFF_XBC = OFF_Z + D_INNER
OFF_GA = OFF_XBC + CONV_CH
OFF_GB = OFF_GA + D_MODEL
OFF_DT = OFF_GB + D_MODEL
DT_W = LANES
N_PROJ = OFF_DT + DT_W
KV_CAT = 2 * N_KV_HEADS * LANES
V_OFF = N_KV_HEADS * LANES
LOG2E = 1.4426950408889634

ROW_TILE = 512
SSD_ROWS = 512
ATT_BLOCK = 128


def _cparams(sem):
    return pltpu.CompilerParams(dimension_semantics=sem, vmem_limit_bytes=VMEM_LIMIT)


def _const_spec(shape):
    n = len(shape)
    return pl.BlockSpec(shape, lambda *_: (0,) * n, pipeline_mode=pl.Buffered(1))


def _dot(a, b):
    return jnp.dot(a, b, preferred_element_type=F32)


def _dot_nt(a, b):
    return lax.dot_general(a, b, (((1,), (1,)), ((), ())), preferred_element_type=F32)


def _dot_tn(a, b):
    return lax.dot_general(a, b, (((0,), (0,)), ((), ())), preferred_element_type=F32)


def _sigmoid(v):
    return 1.0 / (1.0 + jnp.exp(-v))


def _silu(v):
    return v * _sigmoid(v)


def _split3(v):
    hi = v.astype(BF16)
    r = v - hi.astype(F32)
    mid = r.astype(BF16)
    lo = (r - mid.astype(F32)).astype(BF16)
    return hi, mid, lo


def _mod_kernel(c_ref, w_ref, b_ref, o_ref):
    act = _silu(c_ref[...])
    o_ref[...] = jnp.dot(act, w_ref[...], preferred_element_type=F32,
                         precision=lax.Precision.HIGHEST) + b_ref[...]


def _modulation(c_all, w_mod, b_mod):
    rows, d = c_all.shape
    n = w_mod.shape[1]
    tn = 1024
    return pl.pallas_call(
        _mod_kernel,
        grid=(n // tn,),
        in_specs=[pl.BlockSpec((rows, d), lambda j: (0, 0)),
                  pl.BlockSpec((d, tn), lambda j: (0, j)),
                  pl.BlockSpec((1, tn), lambda j: (0, j))],
        out_specs=pl.BlockSpec((rows, tn), lambda j: (0, j)),
        out_shape=jax.ShapeDtypeStruct((rows, n), F32),
        compiler_params=_cparams(("arbitrary",)),
        name="modulation",
    )(c_all, w_mod, b_mod)


def _adaln(x, g_ref, sc_ref, sh_ref):
    ms = jnp.mean(x * x, axis=-1, keepdims=True)
    xn = x * lax.rsqrt(ms + EPS)
    return xn * (g_ref[...] * (1.0 + sc_ref[...])) + sh_ref[...]


def _store_padded_heads(val, col0, kv_ref):
    lane = lax.broadcasted_iota(jnp.int32, (1, LANES), 1)
    is_lo = lane < HEAD_DIM
    for s in range(2):
        slab = val[:, s * LANES:(s + 1) * LANES]
        swapped = pltpu.roll(slab, HEAD_DIM, 1)
        for h, src in ((2 * s, slab), (2 * s + 1, swapped)):
            kv_ref[:, col0 + h * LANES: col0 + (h + 1) * LANES] = jnp.where(is_lo, src, 0.0).astype(BF16)


def _rope_slab(slab, cos, sin_signed, first_half):
    fwd = pltpu.roll(slab, LANES - ROPE_FREQS, 1)
    bwd = pltpu.roll(slab, ROPE_FREQS, 1)
    return slab * cos + jnp.where(first_half, fwd, bwd) * sin_signed


PROJ_PIECE = 1024


def _softplus(v):
    return jnp.maximum(v, 0.0) + jnp.log1p(jnp.exp(-jnp.abs(v)))


def _project_kv(proj, rope, kv_ref):
    k = proj(OFF_K, KV_W)
    if rope is not None:
        k = jnp.concatenate([_rope_slab(k[:, s * LANES:(s + 1) * LANES], *rope) for s in range(2)], axis=1)
    _store_padded_heads(k, 0, kv_ref)
    _store_padded_heads(proj(OFF_V, KV_W), V_OFF, kv_ref)


def _inproj_ctx_kernel(x_ref, sc_ref, sh_ref, g_ref, w_ref, dtb_ref, kv_ref, xbc_ref, dt_ref):
    h = _adaln(x_ref[...], g_ref, sc_ref, sh_ref).astype(BF16)
    proj = lambda off, width: _dot(h, w_ref[:, off:off + width])
    _project_kv(proj, None, kv_ref)
    for c0 in range(0, CONV_CH, PROJ_PIECE):
        xbc_ref[:, c0:c0 + PROJ_PIECE] = proj(OFF_XBC + c0, PROJ_PIECE).astype(BF16)
    dt_ref[...] = _softplus(proj(OFF_DT, DT_W) + dtb_ref[...])


def _inproj_context(xc2, sc, sh, g, w_main, dt_bias, ctx_len):
    t, d = xc2.shape
    tm = ctx_len
    mod_spec = pl.BlockSpec((None, 1, d), lambda i: (0, 0, 0))
    row = lambda w: pl.BlockSpec((tm, w), lambda i: (i, 0))
    sds = lambda w, dt: jax.ShapeDtypeStruct((t, w), dt)
    return pl.pallas_call(
        _inproj_ctx_kernel,
        grid=(t // tm,),
        in_specs=[row(d), mod_spec, mod_spec, _const_spec((1, d)), _const_spec((d, N_PROJ)),
                  _const_spec((1, DT_W))],
        out_specs=[row(KV_CAT), row(CONV_CH), row(DT_W)],
        out_shape=[sds(KV_CAT, BF16), sds(CONV_CH, BF16), sds(DT_W, F32)],
        compiler_params=_cparams(("parallel",)),
        name="inproj_context",
    )(xc2, sc, sh, g, w_main, dt_bias)


def _attn_scores(q_slabs, k_parts):
    half = HEAD_DIM

    def swap_halves(t):
        return jnp.concatenate([t[half:], t[:half]], axis=0)

    t0, t1 = (q.astype(F32).T.astype(BF16) for q in q_slabs)
    q_t = jnp.concatenate([t0, swap_halves(t0), t1, swap_halves(t1)], axis=1)
    return _dot(jnp.concatenate(k_parts, axis=0), q_t)


def _attn_finish(s, v_parts, bias_prev, bias_next, sink_row):
    half = HEAD_DIM
    s_p = s[0:ATT_BLOCK] + bias_prev
    s_c = s[ATT_BLOCK:2 * ATT_BLOCK]
    s_n = s[2 * ATT_BLOCK:3 * ATT_BLOCK] + bias_next
    s_x = s[3 * ATT_BLOCK:]
    red = lambda f, parts: functools.reduce(f, parts)
    m = red(jnp.maximum, [jnp.max(p, axis=0, keepdims=True) for p in (s_p, s_c, s_n, s_x)])
    m = jnp.maximum(m, sink_row)
    e = [jnp.exp2(p - m) for p in (s_p, s_c, s_n, s_x)]
    denom = red(jnp.add, [jnp.sum(p, axis=0, keepdims=True) for p in e]) + jnp.exp2(sink_row - m)
    p_all = jnp.concatenate([p.astype(BF16) for p in e], axis=0)
    v_t = jnp.concatenate(v_parts, axis=0).T[:half]
    out_t = _dot(v_t, p_all) * (1.0 / denom)
    return [jnp.concatenate([out_t[:, (2 * sl) * ATT_BLOCK:(2 * sl + 1) * ATT_BLOCK],
                             out_t[:, (2 * sl + 1) * ATT_BLOCK:(2 * sl + 2) * ATT_BLOCK]], axis=0).T
            for sl in range(2)]


def _inproj_attn_kernel(per_seq, sink_ref, x_ref, sc_ref, sh_ref, g_ref, w_ref, dtb_ref, cos_ref, sin_ref, kvx_ref,
                        z_ref, xbc_ref, dt_ref, ga_ref, gb_ref, ya_ref,
                        q_cur, q_nxt, kv_cur, kv_nxt, kv_tail):
    i = pl.program_id(0)
    blocks = ROW_TILE // ATT_BLOCK
    group = N_Q_HEADS // N_KV_HEADS
    width = group * ATT_BLOCK

    @pl.when(i == 0)
    def _():
        q_cur[...] = jnp.zeros(q_cur.shape, BF16)
        kv_cur[...] = jnp.zeros(kv_cur.shape, BF16)
        kv_tail[...] = jnp.zeros(kv_tail.shape, BF16)

    def parts(sub, kvh, col0):
        qrows = slice(sub * ATT_BLOCK, (sub + 1) * ATT_BLOCK)
        cols = slice(col0 + kvh * LANES, col0 + (kvh + 1) * LANES)
        prev = kv_tail[:, cols] if sub == 0 else kv_cur[(sub - 1) * ATT_BLOCK:sub * ATT_BLOCK, cols]
        nxt = (kv_nxt[0:ATT_BLOCK, cols] if sub == blocks - 1
               else kv_cur[(sub + 1) * ATT_BLOCK:(sub + 2) * ATT_BLOCK, cols])
        return [prev, kv_cur[qrows, cols], nxt, kvx_ref[:, cols]]

    def scores(sub, kvh):
        qrows = slice(sub * ATT_BLOCK, (sub + 1) * ATT_BLOCK)
        q_slabs = [q_cur[qrows, (2 * kvh + sl) * LANES:(2 * kvh + sl + 1) * LANES] for sl in range(2)]
        return _attn_scores(q_slabs, parts(sub, kvh, 0))

    work = [(sub, kvh) for sub in range(blocks) for kvh in range(N_KV_HEADS)]
    s_next = scores(*work[0])

    h = _adaln(x_ref[...], g_ref, sc_ref, sh_ref).astype(BF16)
    proj = lambda off, wid: _dot(h, w_ref[:, off:off + wid])
    cos = cos_ref[...]
    sin = sin_ref[...]
    lane = lax.broadcasted_iota(jnp.int32, (1, LANES), 1)
    first_half = (lane % (2 * ROPE_FREQS)) < ROPE_FREQS
    _project_kv(proj, (cos, sin, first_half), kv_nxt)
    scale = HEAD_DIM ** -0.5 * LOG2E
    q = proj(OFF_Q, ATT_W)
    for s in range(ATT_W // LANES):
        q_nxt[:, s * LANES:(s + 1) * LANES] = (
            _rope_slab(q[:, s * LANES:(s + 1) * LANES], cos, sin, first_half) * scale).astype(BF16)

    piece = PROJ_PIECE // 2

    def store_xbc(c0, wid=piece):
        xbc_ref[:, c0:c0 + wid] = proj(OFF_XBC + c0, wid).astype(BF16)

    def store_z(c0):
        z_ref[:, c0:c0 + piece] = _silu(proj(OFF_Z + c0, piece)).astype(BF16)

    def store_dt():
        dt_ref[...] = _softplus(proj(OFF_DT, DT_W) + dtb_ref[...])

    def store_gate(ref, off, c0):
        ref[:, c0:c0 + piece] = _sigmoid(proj(off + c0, piece)).astype(BF16)

    plain = ([functools.partial(store_xbc, c0) for c0 in range(0, CONV_CH - piece, piece)]
             + [functools.partial(store_xbc, CONV_CH - piece, piece // 2),
                functools.partial(store_xbc, CONV_CH - piece // 2, piece // 2), store_dt])
    gated = ([functools.partial(store_z, c0) for c0 in range(0, D_INNER, piece)]
             + [functools.partial(store_gate, ga_ref, OFF_GA, c0) for c0 in range(0, D_MODEL, piece)]
             + [functools.partial(store_gate, gb_ref, OFF_GB, c0) for c0 in range(0, D_MODEL, piece)])
    segments = [s for pair in zip(plain, gated) for s in pair] + gated[len(plain):] + plain[len(gated):]

    seq_tile = (i + per_seq - 1) % per_seq
    key_j = lax.broadcasted_iota(jnp.int32, (ATT_BLOCK, width), 0)
    lane_w = lax.broadcasted_iota(jnp.int32, (ATT_BLOCK, width), 1)
    query_r = lane_w % ATT_BLOCK
    neg = jnp.float32(-1e30)
    band_prev = key_j >= query_r
    band_next = key_j <= query_r
    lane1 = lax.broadcasted_iota(jnp.int32, (1, width), 1)

    for n, (sub, kvh) in enumerate(work):
        s_cur = s_next
        if n + 1 < len(work):
            s_next = scores(*work[n + 1])
        if segments:
            segments.pop(0)()
        qrows = slice(sub * ATT_BLOCK, (sub + 1) * ATT_BLOCK)
        prev_ok = (seq_tile > 0) if sub == 0 else True
        next_ok = (seq_tile < per_seq - 1) if sub == blocks - 1 else True
        bias_prev = jnp.where(band_prev & prev_ok, 0.0, neg)
        bias_next = jnp.where(band_next & next_ok, 0.0, neg)
        sink_row = jnp.zeros((1, width), F32)
        for g in range(group):
            sink_row = jnp.where(lane1 // ATT_BLOCK == g, sink_ref[kvh * group + g] * LOG2E, sink_row)
        outs = _attn_finish(s_cur, parts(sub, kvh, V_OFF), bias_prev, bias_next, sink_row)
        for sl in range(2):
            ya_ref[qrows, (2 * kvh + sl) * LANES:(2 * kvh + sl + 1) * LANES] = outs[sl].astype(BF16)
    for seg in segments:
        seg()

    kv_tail[...] = kv_cur[ROW_TILE - ATT_BLOCK:, :]
    kv_cur[...] = kv_nxt[...]
    q_cur[...] = q_nxt[...]


def _inproj_attention(x2, sc, sh, g, w_main, dt_bias, cos, sin, kv_ctx, sink, seq_len, ctx_len):
    t, d = x2.shape
    tm = ROW_TILE
    per_seq = seq_len // tm
    n_tiles = t // tm
    cur = lambda i: jnp.minimum(i, n_tiles - 1)
    prev = lambda i: jnp.maximum(i - 1, 0)
    mod_spec = pl.BlockSpec((None, 1, d), lambda i: (cur(i) // per_seq, 0, 0))
    pos_spec = pl.BlockSpec((tm, LANES), lambda i: (cur(i) % per_seq, 0))
    row = lambda w: pl.BlockSpec((tm, w), lambda i: (cur(i), 0))
    sds = lambda w, dt: jax.ShapeDtypeStruct((t, w), dt)
    return pl.pallas_call(
        functools.partial(_inproj_attn_kernel, per_seq),
        grid=(n_tiles + 1,),
        in_specs=[pl.BlockSpec(memory_space=pltpu.SMEM), row(d), mod_spec, mod_spec, _const_spec((1, d)),
                  _const_spec((d, N_PROJ)), _const_spec((1, DT_W)), pos_spec, pos_spec,
                  pl.BlockSpec((ctx_len, KV_CAT), lambda i: (prev(i) // per_seq, 0))],
        out_specs=[row(D_INNER), row(CONV_CH), row(DT_W), row(d), row(d),
                   pl.BlockSpec((tm, ATT_W), lambda i: (prev(i), 0))],
        out_shape=[sds(D_INNER, BF16), sds(CONV_CH, BF16), sds(DT_W, F32), sds(d, BF16), sds(d, BF16),
                   sds(ATT_W, BF16)],
        scratch_shapes=[pltpu.VMEM((tm, ATT_W), BF16), pltpu.VMEM((tm, ATT_W), BF16),
                        pltpu.VMEM((tm, KV_CAT), BF16), pltpu.VMEM((tm, KV_CAT), BF16),
                        pltpu.VMEM((ATT_BLOCK, KV_CAT), BF16)],
        compiler_params=_cparams(("arbitrary",)),
        name="inproj_attention",
    )(sink, x2, sc, sh, g, w_main, dt_bias, cos, sin, kv_ctx)


def _conv_silu(buf_ref, n_blocks, smat_ref, w_ref, b_ref, p_ref, out_ref):
    for m in range(n_blocks):
        r0 = m * CHUNK
        slot = m % 2
        for k in range(CONV_K):
            start = r0 if k <= CONV_K // 2 else r0 + BF16_ROWS
            p_ref[slot, k * CONV_WIN:(k + 1) * CONV_WIN, :] = (
                buf_ref[pl.ds(start, CONV_WIN), :] * w_ref[k:k + 1, :].astype(BF16))
        acc = _dot(smat_ref[...], p_ref[slot]) + b_ref[...]
        out_ref[pl.ds(r0, CHUNK), :] = _silu(acc).astype(out_ref.dtype)


def _tri():
    r = lax.broadcasted_iota(jnp.int32, (CHUNK, CHUNK), 0)
    c = lax.broadcasted_iota(jnp.int32, (CHUNK, CHUNK), 1)
    return r >= c, r <= c


def _cumsum_rows(adts, tri_bf16):
    x3 = jnp.concatenate([t for a in adts for t in _split3(a)], axis=1)
    c = _dot(tri_bf16, x3)
    w = 3 * LANES
    return [c[:, i * w:i * w + LANES] + c[:, i * w + LANES:i * w + 2 * LANES] + c[:, i * w + 2 * LANES:(i + 1) * w]
            for i in range(len(adts))]


def _cumsum_cols(adt_ts, tri_bf16):
    x3 = jnp.concatenate([t for a in adt_ts for t in _split3(a)], axis=0)
    c = _dot(x3, tri_bf16)
    h = 3 * CHUNK
    return [c[i * h:i * h + CHUNK] + c[i * h + CHUNK:i * h + 2 * CHUNK] + c[i * h + 2 * CHUNK:(i + 1) * h]
            for i in range(len(adt_ts))]


def _expand(v, e_ref):
    return _dot(v.astype(BF16), e_ref[...])


def _state_update(state_ref, b_chunk, xs_chunk, w_exp, tot_exp):
    for g in range(N_SSM_GROUPS):
        cols = slice(g * GROUP_W, (g + 1) * GROUP_W)
        xw = (xs_chunk[:, cols].astype(F32) * w_exp[:, cols]).astype(BF16)
        upd = _dot_tn(b_chunk[:, g * D_STATE:(g + 1) * D_STATE], xw)
        state_ref[g] = state_ref[g] * tot_exp[:, cols] + upd


def _neg_a_log2(alog_ref):
    return -jnp.exp(alog_ref[...]) * LOG2E


def _state_weights(acs, dts, e_ref, total_row):
    totals = [a[total_row:total_row + 1, :] for a in acs]
    w = jnp.concatenate([jnp.exp2(t - a) * d for t, a, d in zip(totals, acs, dts)], axis=0)
    pad = jnp.zeros((SUBLANES - len(acs), LANES), F32)
    tot = jnp.concatenate([jnp.exp2(t) for t in totals] + [pad], axis=0)
    return _expand(w, e_ref), _expand(tot, e_ref)


def _chunk_state_update(state_ref, act_ref, c, w_exp, tot_exp):
    rows = pl.ds(c * CHUNK, CHUNK)
    _state_update(state_ref, act_ref[rows, D_INNER:D_INNER + BC_W], act_ref[rows, 0:D_INNER],
                  w_exp[c * CHUNK:(c + 1) * CHUNK], tot_exp[c:c + 1])


def _ctx_state_kernel(n_chunks, xbc_ref, dt_ref, smat_ref, cw_ref, cb_ref, alog_ref, ef_ref, eb_ref,
                      sf_ref, sb_ref, buf_ref, p_ref, act_ref):
    rows = n_chunks * CHUNK
    zeros = jnp.zeros((BF16_ROWS, CONV_CH), BF16)
    buf_ref[0:BF16_ROWS, :] = zeros
    buf_ref[pl.ds(BF16_ROWS, rows), :] = xbc_ref[...]
    buf_ref[pl.ds(BF16_ROWS + rows, BF16_ROWS), :] = zeros
    _conv_silu(buf_ref, n_chunks, smat_ref, cw_ref, cb_ref, p_ref, act_ref)

    a_row = _neg_a_log2(alog_ref)
    lower, upper = _tri()
    dts = [dt_ref[pl.ds(c * CHUNK, CHUNK), :] for c in range(n_chunks)]
    adts = [d * a_row for d in dts]
    wf_exp, totf_exp = _state_weights(_cumsum_rows(adts, lower.astype(BF16)), dts, ef_ref, CHUNK - 1)
    wb_exp, totb_exp = _state_weights(_cumsum_rows(adts, upper.astype(BF16)), dts, eb_ref, 0)
    sf_ref[...] = jnp.zeros(sf_ref.shape, F32)
    sb_ref[...] = jnp.zeros(sb_ref.shape, F32)
    for c in range(n_chunks):
        _chunk_state_update(sf_ref, act_ref, c, wf_exp, totf_exp)
    for c in reversed(range(n_chunks)):
        _chunk_state_update(sb_ref, act_ref, c, wb_exp, totb_exp)


def _conv_scratch(rows):
    return [pltpu.VMEM((rows + 2 * BF16_ROWS, CONV_CH), BF16),
            pltpu.VMEM((2, CONV_K * CONV_WIN, CONV_CH), BF16)]


def _ctx_states(xbc_c, dt_c, smat, conv_w8, conv_b, alog, ef, eb, batch, ctx_len):
    n_chunks = ctx_len // CHUNK
    state = jax.ShapeDtypeStruct((batch, N_SSM_GROUPS, D_STATE, GROUP_W), F32)
    st_spec = pl.BlockSpec((None, N_SSM_GROUPS, D_STATE, GROUP_W), lambda b: (b, 0, 0, 0))
    return pl.pallas_call(
        functools.partial(_ctx_state_kernel, n_chunks),
        grid=(batch,),
        in_specs=[pl.BlockSpec((ctx_len, CONV_CH), lambda b: (b, 0)),
                  pl.BlockSpec((ctx_len, DT_W), lambda b: (b, 0)),
                  _const_spec(smat.shape), _const_spec((SUBLANES, CONV_CH)), _const_spec((1, CONV_CH)),
                  _const_spec((1, LANES)), _const_spec(ef.shape), _const_spec(eb.shape)],
        out_specs=[st_spec, st_spec],
        out_shape=[state, state],
        scratch_shapes=_conv_scratch(ctx_len) + [pltpu.VMEM((ctx_len, CONV_CH), BF16)],
        compiler_params=_cparams(("parallel",)),
        name="ssd_context_states",
    )(xbc_c, dt_c, smat, conv_w8, conv_b, alog, ef, eb)


def _bwd_sweep_kernel(n_blk, cur_ref, prev_ref, next_ref, dt_ref, smat_ref, cw_ref, cb_ref, alog_ref, eb_ref,
                      sb0_ref, act_ref, sin_ref, buf_ref, p_ref, state_ref):
    j = pl.program_id(1)
    blk = n_blk - 1 - j
    rows = SSD_ROWS
    n_chunks = rows // CHUNK

    @pl.when(j == 0)
    def _():
        state_ref[...] = sb0_ref[...]

    zeros = jnp.zeros((BF16_ROWS, CONV_CH), BF16)
    buf_ref[0:BF16_ROWS, :] = jnp.where(blk > 0, prev_ref[...], zeros)
    buf_ref[pl.ds(BF16_ROWS, rows), :] = cur_ref[...]
    buf_ref[pl.ds(BF16_ROWS + rows, BF16_ROWS), :] = jnp.where(blk < n_blk - 1, next_ref[...], zeros)
    _conv_silu(buf_ref, n_chunks, smat_ref, cw_ref, cb_ref, p_ref, act_ref)

    a_row = _neg_a_log2(alog_ref)
    _, upper = _tri()
    dts = [dt_ref[pl.ds(c * CHUNK, CHUNK), :] for c in range(n_chunks)]
    acs_b = _cumsum_rows([d * a_row for d in dts], upper.astype(BF16))
    w_exp, tot_exp = _state_weights(acs_b, dts, eb_ref, 0)
    for c in reversed(range(n_chunks)):
        sin_ref[c] = state_ref[...].astype(BF16)
        _chunk_state_update(state_ref, act_ref, c, w_exp, tot_exp)


def _bwd_sweep(xbc, dt, smat, conv_w8, conv_b, alog, eb, sb0, batch, seq_len):
    t = xbc.shape[0]
    rows = SSD_ROWS
    n_blk = seq_len // rows
    cpb = rows // CHUNK
    halo_per_blk = rows // BF16_ROWS
    n_halo = t // BF16_ROWS

    def blk_of(b, j):
        return b * n_blk + (n_blk - 1 - j)

    return pl.pallas_call(
        functools.partial(_bwd_sweep_kernel, n_blk),
        grid=(batch, n_blk),
        in_specs=[pl.BlockSpec((rows, CONV_CH), lambda b, j: (blk_of(b, j), 0)),
                  pl.BlockSpec((BF16_ROWS, CONV_CH),
                               lambda b, j: (jnp.maximum(blk_of(b, j) * halo_per_blk - 1, 0), 0)),
                  pl.BlockSpec((BF16_ROWS, CONV_CH),
                               lambda b, j: (jnp.minimum((blk_of(b, j) + 1) * halo_per_blk, n_halo - 1), 0)),
                  pl.BlockSpec((rows, DT_W), lambda b, j: (blk_of(b, j), 0)),
                  _const_spec(smat.shape), _const_spec((SUBLANES, CONV_CH)), _const_spec((1, CONV_CH)),
                  _const_spec((1, LANES)), _const_spec(eb.shape),
                  pl.BlockSpec((None, N_SSM_GROUPS, D_STATE, GROUP_W), lambda b, j: (b, 0, 0, 0))],
        out_specs=[pl.BlockSpec((rows, CONV_CH), lambda b, j: (blk_of(b, j), 0)),
                   pl.BlockSpec((cpb, N_SSM_GROUPS, D_STATE, GROUP_W), lambda b, j: (blk_of(b, j), 0, 0, 0))],
        out_shape=[jax.ShapeDtypeStruct((t, CONV_CH), BF16),
                   jax.ShapeDtypeStruct((t // CHUNK, N_SSM_GROUPS, D_STATE, GROUP_W), BF16)],
        scratch_shapes=_conv_scratch(rows) + [pltpu.VMEM((N_SSM_GROUPS, D_STATE, GROUP_W), F32)],
        compiler_params=_cparams(("parallel", "arbitrary")),
        name="ssd_backward_sweep",
    )(xbc, xbc, xbc, dt, smat, conv_w8, conv_b, alog, eb, sb0)


def _fwd_sweep_kernel(act_ref, dt_ref, z_ref, sbin_ref, sf0_ref, alog_ref, ef2_ref, dskip_ref, ng_ref,
                      y_ref, state_ref, ybuf_ref):
    j = pl.program_id(1)

    @pl.when(j == 0)
    def _():
        state_ref[...] = sf0_ref[...]

    a_row = _neg_a_log2(alog_ref)
    lower, upper = _tri()
    lower_bf16 = lower.astype(BF16)
    upper_bf16 = upper.astype(BF16)
    diag = lower & upper
    lane = lax.broadcasted_iota(jnp.int32, (1, LANES), 1)
    lane_lo = lane < SSM_HEAD_DIM
    heads_per_group = N_SSM_HEADS // N_SSM_GROUPS

    n_chunks = SSD_ROWS // CHUNK
    dts = [dt_ref[pl.ds(c * CHUNK, CHUNK), :] for c in range(n_chunks)]
    adts = [d * a_row for d in dts]
    acs_f_all = _cumsum_rows(adts, lower_bf16)
    acs_b_all = _cumsum_rows(adts, upper_bf16)
    adt_ts = [a.T for a in adts]
    acst_f_all = _cumsum_cols(adt_ts, upper_bf16)
    acst_b_all = _cumsum_cols(adt_ts, lower_bf16)
    w_all, tot_all = _state_weights(acs_f_all, dts, ef2_ref, CHUNK - 1)

    for c in range(n_chunks):
        rows = pl.ds(c * CHUNK, CHUNK)
        crows = slice(c * CHUNK, (c + 1) * CHUNK)
        acs_f, acs_b = acs_f_all[c], acs_b_all[c]
        dt_t = dts[c].T
        log2_dt_t = jnp.log2(dt_t)
        row_f = acst_f_all[c] - log2_dt_t
        row_b = acst_b_all[c] - log2_dt_t
        diag_log = jnp.log2(dt_t[0:N_SSM_HEADS] + dt_t[N_SSM_HEADS:2 * N_SSM_HEADS])

        b_chunk = act_ref[rows, D_INNER:D_INNER + BC_W]
        c_chunk = act_ref[rows, D_INNER + BC_W:CONV_CH]

        for g in range(N_SSM_GROUPS):
            c_g = c_chunk[:, g * D_STATE:(g + 1) * D_STATE]
            cb = _dot_nt(c_g, b_chunk[:, g * D_STATE:(g + 1) * D_STATE])
            c_g32 = c_g.astype(F32)
            for pair in range(g * heads_per_group // 2, (g + 1) * heads_per_group // 2):
                pcols = slice((pair % (heads_per_group // 2)) * LANES, (pair % (heads_per_group // 2) + 1) * LANES)
                xs_pair = act_ref[rows, pair * LANES:(pair + 1) * LANES]
                sf_pair = state_ref[g, :, pcols].astype(BF16)
                sb_pair = sbin_ref[c, g, :, pcols]
                outs = []
                for h in (2 * pair, 2 * pair + 1):
                    hb = N_SSM_HEADS + h
                    col_f = jnp.broadcast_to(acs_f[:, h:h + 1], (CHUNK, CHUNK))
                    col_b = jnp.broadcast_to(acs_b[:, hb:hb + 1], (CHUNK, CHUNK))
                    seg = jnp.where(diag, diag_log[h:h + 1, :],
                                    jnp.where(lower, col_f - row_f[h:h + 1, :], col_b - row_b[hb:hb + 1, :]))
                    outs.append(_dot((cb * jnp.exp2(seg)).astype(BF16), xs_pair)
                                + _dot((c_g32 * jnp.exp2(col_f)).astype(BF16), sf_pair)
                                + _dot((c_g32 * jnp.exp2(col_b)).astype(BF16), sb_pair))
                ybuf_ref[:, pair * LANES:(pair + 1) * LANES] = jnp.where(lane_lo, outs[0], outs[1])

        for g in range(N_SSM_GROUPS):
            cols = slice(g * GROUP_W, (g + 1) * GROUP_W)
            y = ybuf_ref[:, cols] + dskip_ref[:, cols] * act_ref[rows, cols].astype(F32)
            y = y * z_ref[rows, cols].astype(F32)
            ms = jnp.mean(y * y, axis=-1, keepdims=True)
            y_ref[rows, cols] = (y * lax.rsqrt(ms + EPS) * ng_ref[:, cols]).astype(BF16)

        _state_update(state_ref, b_chunk, act_ref[rows, 0:D_INNER], w_all[crows], tot_all[c:c + 1])


def _fwd_sweep(act, dt, z, sb_in, sf0, alog, ef2, dskip, norm_g, batch, seq_len):
    t = act.shape[0]
    rows = SSD_ROWS
    n_blk = seq_len // rows
    cpb = rows // CHUNK
    blk = lambda b, j: b * n_blk + j
    return pl.pallas_call(
        _fwd_sweep_kernel,
        grid=(batch, n_blk),
        in_specs=[pl.BlockSpec((rows, CONV_CH), lambda b, j: (blk(b, j), 0)),
                  pl.BlockSpec((rows, DT_W), lambda b, j: (blk(b, j), 0)),
                  pl.BlockSpec((rows, D_INNER), lambda b, j: (blk(b, j), 0)),
                  pl.BlockSpec((cpb, N_SSM_GROUPS, D_STATE, GROUP_W), lambda b, j: (blk(b, j), 0, 0, 0)),
                  pl.BlockSpec((None, N_SSM_GROUPS, D_STATE, GROUP_W), lambda b, j: (b, 0, 0, 0)),
                  _const_spec((1, LANES)), _const_spec(ef2.shape),
                  _const_spec((1, D_INNER)), _const_spec((1, D_INNER))],
        out_specs=pl.BlockSpec((rows, D_INNER), lambda b, j: (blk(b, j), 0)),
        out_shape=jax.ShapeDtypeStruct((t, D_INNER), BF16),
        scratch_shapes=[pltpu.VMEM((N_SSM_GROUPS, D_STATE, GROUP_W), F32),
                        pltpu.VMEM((CHUNK, D_INNER), F32)],
        compiler_params=_cparams(("parallel", "arbitrary")),
        name="ssd_forward_sweep",
    )(act, dt, z, sb_in, sf0, alog, ef2, dskip, norm_g)


FF_SPLIT = 1280


def _merge_ffn_kernel(x_ref, ya_ref, ys_ref, ga_ref, gb_ref, g1_ref, sc_ref, sh_ref, g2_ref, ng_ref, fg_ref,
                      woa_ref, wob_ref, wout_ref, win_ref, wffo_ref, o_ref):
    half_rows = ROW_TILE // 2
    halves = [slice(k * half_rows, (k + 1) * half_rows) for k in range(2)]

    def merge(r):
        a = _dot(ya_ref[r, :], woa_ref[...])
        b = _dot(ys_ref[r, :], wob_ref[...])
        merged = (ga_ref[r, :].astype(F32) * a + gb_ref[r, :].astype(F32) * b).astype(BF16)
        return x_ref[r, :] + g1_ref[...] * _dot(merged, wout_ref[...])

    def swiglu(h):
        acc = None
        for c0, c1 in ((0, FF_SPLIT), (FF_SPLIT, D_FF)):
            gate = _dot(h, win_ref[:, c0:c1])
            up = _dot(h, win_ref[:, D_FF + c0:D_FF + c1])
            part = _dot((_silu(gate) * up).astype(BF16), wffo_ref[c0:c1, :])
            acc = part if acc is None else acc + part
        return acc

    x1 = [merge(r) for r in halves]
    hs = [_adaln(v, ng_ref, sc_ref, sh_ref).astype(BF16) for v in x1]
    ff = [swiglu(h) for h in hs]
    for r, v, f in zip(halves, x1, ff):
        x2 = v + g2_ref[...] * f
        ms = jnp.mean(x2 * x2, axis=-1, keepdims=True)
        o_ref[r, :] = x2 * lax.rsqrt(ms + EPS) * fg_ref[...]


def _merge_ffn(x2, y_att, y_ssm, ga, gb, g1, sc2, sh2, g2, norm_g, final_g, w_oa, w_ob, w_out, w_ffn_in, w_ffn_out,
               seq_len):
    t, d = x2.shape
    tm = ROW_TILE
    per_seq = seq_len // tm
    row = lambda w: pl.BlockSpec((tm, w), lambda i: (i, 0))
    mod_spec = pl.BlockSpec((None, 1, d), lambda i: (i // per_seq, 0, 0))
    return pl.pallas_call(
        _merge_ffn_kernel,
        grid=(t // tm,),
        in_specs=[row(d), row(ATT_W), row(D_INNER), row(d), row(d), mod_spec, mod_spec, mod_spec, mod_spec,
                  _const_spec((1, d)), _const_spec((1, d)),
                  _const_spec(w_oa.shape), _const_spec(w_ob.shape), _const_spec(w_out.shape),
                  _const_spec(w_ffn_in.shape), _const_spec(w_ffn_out.shape)],
        out_specs=row(d),
        out_shape=jax.ShapeDtypeStruct((t, d), F32),
        compiler_params=_cparams(("parallel",)),
        name="merge_ffn_final_norm",
    )(x2, y_att, y_ssm, ga, gb, g1, sc2, sh2, g2, norm_g, final_g, w_oa, w_ob, w_out, w_ffn_in, w_ffn_out)


def _rope_tables(seq_len):
    rows = seq_len // GRID_W
    row = jnp.repeat(jnp.arange(rows), GRID_W).astype(F32)
    col = jnp.tile(jnp.arange(GRID_W), rows).astype(F32)
    inv = ROPE_BASE ** (-jnp.arange(ROPE_FREQS, dtype=F32) / ROPE_FREQS)
    ang_r = row[:, None] * inv
    ang_c = col[:, None] * inv
    ang = jnp.concatenate([ang_r, ang_r, ang_c, ang_c] * 2, axis=-1)
    sign = np.where((np.arange(LANES) % (2 * ROPE_FREQS)) < ROPE_FREQS, -1.0, 1.0).astype(np.float32)
    return jnp.cos(ang), jnp.sin(ang) * jnp.asarray(sign)[None, :]


def _expand_matrix(first_row):
    e = np.zeros((LANES, D_INNER), np.float32)
    for h in range(N_SSM_HEADS):
        e[first_row + h, h * SSM_HEAD_DIM:(h + 1) * SSM_HEAD_DIM] = 1.0
    return jnp.asarray(e, dtype=BF16)


def _conv_shift_matrix():
    s = np.zeros((CHUNK, CONV_K * CONV_WIN), np.float32)
    for k in range(CONV_K):
        off = k + BF16_ROWS - CONV_K // 2 if k <= CONV_K // 2 else k - CONV_K // 2
        for t in range(CHUNK):
            s[t, k * CONV_WIN + t + off] = 1.0
    return jnp.asarray(s, dtype=BF16)


def kernel(x, c, ctx, c_ctx, w_mod, b_mod, norm1_g, w_in, attn_sink, conv_w, conv_b, a_log_f, a_log_b,
           dt_bias_f, dt_bias_b, d_skip, ssm_norm_g, w_oa, w_ob, w_out, norm2_g, w_ffn_in, w_ffn_out, final_g):
    batch, seq_len, d = x.shape
    ctx_len = ctx.shape[1]
    assert w_mod.shape[0] == 1 and d == D_MODEL
    assert seq_len % SSD_ROWS == 0 and seq_len % ROW_TILE == 0 and seq_len % GRID_W == 0
    assert ctx_len % CHUNK == 0

    pad_rows = (-(batch + 1)) % SUBLANES
    c_all = jnp.concatenate([c, c_ctx[None, :], jnp.zeros((pad_rows, d), F32)], axis=0)
    layer = lambda p: p.reshape(p.shape[1:])
    mod = _modulation(c_all, layer(w_mod), b_mod[0][None, :])
    mod_b = mod[:batch].reshape(batch, N_MOD, 1, d)
    sh1, sc1, g1, sh2, sc2, g2 = (mod_b[:, k] for k in range(N_MOD))
    mod_c = mod[batch:batch + 1].reshape(1, N_MOD, 1, d)
    sh1c, sc1c = mod_c[:, 0], mod_c[:, 1]

    w = layer(w_in).astype(BF16)
    o = np.cumsum([0, ATT_W, KV_W, KV_W, D_INNER, CONV_CH, N_SSM_HEADS, N_SSM_HEADS, D_MODEL, D_MODEL])
    w_main = jnp.concatenate(
        [w[:, :o[5]], w[:, o[7]:o[9]], w[:, o[5]:o[7]], jnp.zeros((d, DT_W - 2 * N_SSM_HEADS), BF16)], axis=1)
    pad_heads = jnp.zeros((DT_W - 2 * N_SSM_HEADS,), F32)
    dt_bias = jnp.concatenate([dt_bias_f[0], dt_bias_b[0], pad_heads])[None, :]
    alog = jnp.concatenate([a_log_f[0], a_log_b[0], pad_heads])[None, :]
    conv_w8 = jnp.concatenate([conv_w[0], jnp.zeros((SUBLANES - CONV_K, CONV_CH), F32)], axis=0)
    conv_b1 = conv_b[0][None, :]
    cos, sin = _rope_tables(seq_len)
    ef2 = _expand_matrix(0)
    eb2 = _expand_matrix(N_SSM_HEADS)
    g_norm1 = norm1_g[0][None, :]

    x2 = x.reshape(batch * seq_len, d)
    xc2 = ctx.reshape(batch * ctx_len, d)

    kv_c, xbc_c, dt_c = _inproj_context(xc2, sc1c, sh1c, g_norm1, w_main, dt_bias, ctx_len)
    z, xbc, dt, ga, gb, y_att = _inproj_attention(x2, sc1, sh1, g_norm1, w_main, dt_bias, cos, sin, kv_c,
                                                  attn_sink[0], seq_len, ctx_len)

    smat = _conv_shift_matrix()
    sf0, sb0 = _ctx_states(xbc_c, dt_c, smat, conv_w8, conv_b1, alog, ef2, eb2, batch, ctx_len)
    act, sb_in = _bwd_sweep(xbc, dt, smat, conv_w8, conv_b1, alog, eb2, sb0, batch, seq_len)
    dskip = jnp.repeat(d_skip[0], SSM_HEAD_DIM)[None, :]
    y_ssm = _fwd_sweep(act, dt, z, sb_in, sf0, alog, ef2, dskip, ssm_norm_g[0][None, :], batch, seq_len)

    out = _merge_ffn(x2, y_att, y_ssm, ga, gb, g1, sc2, sh2, g2, norm2_g[0][None, :], final_g[None, :],
                     layer(w_oa).astype(BF16), layer(w_ob).astype(BF16), layer(w_out).astype(BF16),
                     layer(w_ffn_in).astype(BF16), layer(w_ffn_out).astype(BF16), seq_len)
    return out.reshape(batch, seq_len, d)
```
